```python
import jax, jax.numpy as jnp
from jax import lax
import numpy as np

D_MODEL = 4096
BATCH = 2
SEQ = 8192
DEPTH = 2

DEEPNORM_ALPHA = (2.0 * DEPTH) ** 0.25
DEEPNORM_BETA = (8.0 * DEPTH) ** -0.25
LN_EPS = 1e-5
RMS_EPS = 1e-5

POOL_WINDOWS = (2, 4, 8, 16)
POOL_GROUPS = 4
D_POOL = D_MODEL // 2
POOL_GROUP_DIM = D_POOL // POOL_GROUPS

SSD_HEADDIM = 64
D_SSD = (3 * D_MODEL) // 2
SSD_HEADS = D_SSD // SSD_HEADDIM
SSD_GROUPS = 8
SSD_STATE = 128
SSD_CONV = 4
SSD_CHUNK = 256
D_SSD_CONV = D_SSD + 2 * SSD_GROUPS * SSD_STATE
L0_IN = D_POOL + D_SSD + D_SSD_CONV + SSD_HEADS
L0_MIX = D_POOL + D_SSD

D_CONV = D_MODEL // 2
CONV_WIDTH = 3

SB_HEADS = 16
SB_HEADDIM = 128
D_SB = SB_HEADS * SB_HEADDIM
SB_BLOCK = 128
L1_IN = 3 * D_CONV + 3 * D_SB
L1_MIX = D_CONV + D_SB

N_EXPERTS = 32
TOP_K = 4
D_EXPERT = 512
SWIGLU_LIMIT = 7.0
SWIGLU_ALPHA = 1.702

kernel_name = "pool_ssd_shortconv_stickbreak_moe_deepnorm"


def layer_norm(x, g, b):
    xf = x.astype(jnp.float32)
    mu = jnp.mean(xf, axis=-1, keepdims=True)
    xc = xf - mu
    var = jnp.mean(xc * xc, axis=-1, keepdims=True)
    return (xc * lax.rsqrt(var + LN_EPS) * g + b).astype(x.dtype)


def causal_depthwise_conv(x, w):
    k, c = w.shape
    return lax.conv_general_dilated(
        x, w[:, None, :].astype(x.dtype), window_strides=(1,), padding=[(k - 1, 0)],
        dimension_numbers=("NWC", "WIO", "NWC"), feature_group_count=c)


def pool_mixer(u, pool_w, pool_scale):
    b, s, _ = u.shape
    ug = u.reshape(b, s, POOL_GROUPS, POOL_GROUP_DIM)
    cs = jnp.cumsum(ug.astype(jnp.float32), axis=1)
    pos = jnp.arange(1, s + 1, dtype=jnp.float32)
    outs = []
    for g, w in enumerate(POOL_WINDOWS):
        c = cs[:, :, g]
        prev = jnp.pad(c, ((0, 0), (w, 0), (0, 0)))[:, :s]
        mean = (c - prev) / jnp.minimum(pos, float(w))[None, :, None]
        outs.append(mean.astype(u.dtype) - ug[:, :, g])
    m = jnp.stack(outs, axis=2)
    y = jnp.einsum("bsgc,gcd->bsgd", m, pool_w)
    return y.reshape(b, s, D_POOL) * pool_scale


def ssd_chunked_scan(xh, dt, a, bmat, cmat):
    b, s, h, p = xh.shape
    g, n = bmat.shape[2], bmat.shape[3]
    e = h // g
    pad = (-s) % SSD_CHUNK
    if pad:
        padw = lambda t: jnp.pad(t, [(0, 0), (0, pad)] + [(0, 0)] * (t.ndim - 2))
        xh, dt, bmat, cmat = padw(xh), padw(dt), padw(bmat), padw(cmat)
    nc = (s + pad) // SSD_CHUNK
    ln = SSD_CHUNK
    xdt = (xh.astype(jnp.float32) * dt[..., None]).reshape(b, nc, ln, g, e, p)
    da = (dt * a).reshape(b, nc, ln, g, e)
    bc = bmat.reshape(b, nc, ln, g, n)
    cc = cmat.reshape(b, nc, ln, g, n)
    xdt, da, bc, cc = (jnp.moveaxis(t, 1, 0) for t in (xdt, da, bc, cc))
    causal = jnp.tril(jnp.ones((ln, ln), dtype=bool))[None, :, :, None, None]

    def step(state, inp):
        xc, dac, bcc, ccc = inp
        acum = jnp.cumsum(dac, axis=1)
        seg = acum[:, :, None] - acum[:, None, :]
        decay = jnp.exp(jnp.where(causal, seg, -jnp.inf))
        cb = jnp.einsum("btgn,bsgn->btsg", ccc, bcc)
        y_diag = jnp.einsum("btsg,btsge,bsgep->btgep", cb, decay, xc)
        y_off = jnp.einsum("btgn,bgepn->btgep", ccc, state) * jnp.exp(acum)[..., None]
        to_end = jnp.exp(acum[:, -1:] - acum)
        new_state = (state * jnp.exp(acum[:, -1])[..., None, None]
                     + jnp.einsum("bsgn,bsge,bsgep->bgepn", bcc, to_end, xc))
        return new_state, y_diag + y_off

    state0 = jnp.zeros((b, g, e, p, n), jnp.float32)
    _, y = lax.scan(step, state0, (xdt, da, bc, cc))
    return jnp.moveaxis(y, 0, 1).reshape(b, nc * ln, h, p)[:, :s]


def ssd_mixer(z, xbc, dt_raw, conv_w, conv_b, dt_bias, a_log, d_skip, norm_w):
    b, s, _ = z.shape
    xbc = jax.nn.silu(causal_depthwise_conv(xbc, conv_w) + conv_b)
    gn = SSD_GROUPS * SSD_STATE
    xs, bm, cm = jnp.split(xbc, [D_SSD, D_SSD + gn], axis=-1)
    xh = xs.reshape(b, s, SSD_HEADS, SSD_HEADDIM)
    dt = jax.nn.softplus(dt_raw.astype(jnp.float32) + dt_bias.astype(jnp.float32))
    a = -jnp.exp(a_log.astype(jnp.float32))
    y = ssd_chunked_scan(xh, dt, a, bm.reshape(b, s, SSD_GROUPS, SSD_STATE),
                         cm.reshape(b, s, SSD_GROUPS, SSD_STATE))
    y = y + d_skip.astype(jnp.float32)[:, None] * xh.astype(jnp.float32)
    y = y.reshape(b, s, D_SSD) * jax.nn.silu(z.astype(jnp.float32))
    yg = y.reshape(b, s, SSD_GROUPS, D_SSD // SSD_GROUPS)
    yg = yg * lax.rsqrt(jnp.mean(yg * yg, axis=-1, keepdims=True) + RMS_EPS)
    return (yg.reshape(b, s, D_SSD) * norm_w).astype(z.dtype)


def pool_ssd_mixer(x, w_in, conv_w, conv_b, dt_bias, a_log, d_skip, ssm_norm_w,
                   pool_w, pool_scale, w_out):
    h = x @ w_in
    u, z, xbc, dt_raw = jnp.split(h, [D_POOL, D_POOL + D_SSD, D_POOL + D_SSD + D_SSD_CONV], axis=-1)
    y_pool = pool_mixer(u, pool_w, pool_scale)
    y_ssd = ssd_mixer(z, xbc, dt_raw, conv_w, conv_b, dt_bias, a_log, d_skip, ssm_norm_w)
    return jnp.concatenate([y_pool, y_ssd], axis=-1) @ w_out


def stick_breaking_attention(q, k, v):
    b, s, h, dh = q.shape
    nb = s // SB_BLOCK
    scale = dh ** -0.5
    qb = q.reshape(b, nb, SB_BLOCK, h, dh).transpose(1, 0, 3, 2, 4)
    kpos = jnp.arange(s)

    def block(inp):
        qi, i = inp
        zs = jnp.einsum("bhqd,bkhd->bhqk", qi, k).astype(jnp.float32) * scale
        qpos = i * SB_BLOCK + jnp.arange(SB_BLOCK)
        mask = kpos[None, :] < qpos[:, None]
        log_stay = jnp.where(mask, -jax.nn.softplus(zs), 0.0)
        after = lax.cumsum(log_stay, axis=3, reverse=True) - log_stay
        w = jnp.where(mask, jnp.exp(jax.nn.log_sigmoid(zs) + after), 0.0)
        return jnp.einsum("bhqk,bkhd->bqhd", w.astype(v.dtype), v)

    o = lax.map(block, (qb, jnp.arange(nb)))
    return o.transpose(1, 0, 2, 3, 4).reshape(b, s, h * dh)


def shortconv_sb_mixer(x, w_in, conv_w, w_out):
    b, s, _ = x.shape
    h = x @ w_in
    c0 = D_CONV
    bg, cg, xin, q, k, v = jnp.split(
        h, [c0, 2 * c0, 3 * c0, 3 * c0 + D_SB, 3 * c0 + 2 * D_SB], axis=-1)
    y_conv = bg * causal_depthwise_conv(cg * xin, conv_w)
    hs = (b, s, SB_HEADS, SB_HEADDIM)
    y_sb = stick_breaking_attention(q.reshape(hs), k.reshape(hs), v.reshape(hs))
    return jnp.concatenate([y_conv, y_sb], axis=-1) @ w_out


def moe_ffn(x, router_w, router_b, w1, b1, w2, b2):
    b, s, d = x.shape
    t = x.reshape(b * s, d)
    logits = (t @ router_w + router_b).astype(jnp.float32)
    top_val, top_idx = lax.top_k(logits, TOP_K)
    top_w = jax.nn.softmax(top_val, axis=-1)
    gate = jnp.einsum("tk,tke->te", top_w, jax.nn.one_hot(top_idx, N_EXPERTS, dtype=jnp.float32))

    def expert(acc, inp):
        w1e, b1e, w2e, b2e, ge = inp
        hh = t @ w1e + b1e
        glu = jnp.minimum(hh[:, :D_EXPERT], SWIGLU_LIMIT)
        lin = jnp.clip(hh[:, D_EXPERT:], -SWIGLU_LIMIT, SWIGLU_LIMIT)
        act = glu * jax.nn.sigmoid(SWIGLU_ALPHA * glu) * (lin + 1.0)
        out = act @ w2e + b2e
        return acc + ge[:, None] * out.astype(jnp.float32), None

    acc0 = jnp.zeros((b * s, d), jnp.float32)
    acc, _ = lax.scan(expert, acc0, (w1, b1, w2, b2, gate.T))
    return acc.reshape(b, s, d).astype(x.dtype)


def setup_inputs(seed: int = 0) -> dict:
    key = jax.random.key(seed)
    ks = iter(jax.random.split(key, 64))
    f32 = jnp.float32

    def nrm(shape, scale):
        return jax.random.normal(next(ks), shape, f32) * scale

    def unif(shape, lo, hi):
        return jax.random.uniform(next(ks), shape, f32, lo, hi)

    d = D_MODEL
    out = {"x": nrm((BATCH, SEQ, d), 1.0)}
    dt0 = jnp.exp(unif((SSD_HEADS,), float(np.log(1e-3)), float(np.log(1e-1))))
    out["l0_w_in"] = nrm((d, L0_IN), d ** -0.5)
    out["l0_conv_w"] = nrm((SSD_CONV, D_SSD_CONV), SSD_CONV ** -0.5)
    out["l0_conv_b"] = nrm((D_SSD_CONV,), 0.02)
    out["l0_dt_bias"] = dt0 + jnp.log(-jnp.expm1(-dt0))
    out["l0_a_log"] = jnp.log(unif((SSD_HEADS,), 1.0, 16.0))
    out["l0_d_skip"] = 1.0 + nrm((SSD_HEADS,), 0.1)
    out["l0_ssm_norm_w"] = 1.0 + nrm((D_SSD,), 0.02)
    out["l0_pool_w"] = nrm((POOL_GROUPS, POOL_GROUP_DIM, POOL_GROUP_DIM), POOL_GROUP_DIM ** -0.5)
    out["l0_pool_scale"] = 1.0 + nrm((D_POOL,), 0.02)
    out["l0_w_out"] = nrm((L0_MIX, d), DEEPNORM_BETA * L0_MIX ** -0.5)
    out["l0_ln_mix_g"] = 1.0 + nrm((d,), 0.02)
    out["l0_ln_mix_b"] = nrm((d,), 0.02)
    out["l0_router_w"] = nrm((d, N_EXPERTS), d ** -0.5)
    out["l0_router_b"] = nrm((N_EXPERTS,), 0.01)
    out["l0_w1"] = nrm((N_EXPERTS, d, 2 * D_EXPERT), d ** -0.5)
    out["l0_b1"] = nrm((N_EXPERTS, 2 * D_EXPERT), 0.01)
    out["l0_w2"] = nrm((N_EXPERTS, D_EXPERT, d), DEEPNORM_BETA * D_EXPERT ** -0.5)
    out["l0_b2"] = nrm((N_EXPERTS, d), 0.01)
    out["l0_ln_ffn_g"] = 1.0 + nrm((d,), 0.02)
    out["l0_ln_ffn_b"] = nrm((d,), 0.02)
    out["l1_w_in"] = nrm((d, L1_IN), d ** -0.5)
    out["l1_conv_w"] = nrm((CONV_WIDTH, D_CONV), CONV_WIDTH ** -0.5)
    out["l1_w_out"] = nrm((L1_MIX, d), DEEPNORM_BETA * L1_MIX ** -0.5)
    out["l1_ln_mix_g"] = 1.0 + nrm((d,), 0.02)
    out["l1_ln_mix_b"] = nrm((d,), 0.02)
    out["l1_router_w"] = nrm((d, N_EXPERTS), d ** -0.5)
    out["l1_router_b"] = nrm((N_EXPERTS,), 0.01)
    out["l1_w1"] = nrm((N_EXPERTS, d, 2 * D_EXPERT), d ** -0.5)
    out["l1_b1"] = nrm((N_EXPERTS, 2 * D_EXPERT), 0.01)
    out["l1_w2"] = nrm((N_EXPERTS, D_EXPERT, d), DEEPNORM_BETA * D_EXPERT ** -0.5)
    out["l1_b2"] = nrm((N_EXPERTS, d), 0.01)
    out["l1_ln_ffn_g"] = 1.0 + nrm((d,), 0.02)
    out["l1_ln_ffn_b"] = nrm((d,), 0.02)
    return out


def reference(x,
              l0_w_in, l0_conv_w, l0_conv_b, l0_dt_bias, l0_a_log, l0_d_skip, l0_ssm_norm_w,
              l0_pool_w, l0_pool_scale, l0_w_out, l0_ln_mix_g, l0_ln_mix_b,
              l0_router_w, l0_router_b, l0_w1, l0_b1, l0_w2, l0_b2, l0_ln_ffn_g, l0_ln_ffn_b,
              l1_w_in, l1_conv_w, l1_w_out, l1_ln_mix_g, l1_ln_mix_b,
              l1_router_w, l1_router_b, l1_w1, l1_b1, l1_w2, l1_b2, l1_ln_ffn_g, l1_ln_ffn_b):
    mix_params = (
        (l0_w_in, l0_conv_w, l0_conv_b, l0_dt_bias, l0_a_log, l0_d_skip, l0_ssm_norm_w,
         l0_pool_w, l0_pool_scale, l0_w_out),
        (l1_w_in, l1_conv_w, l1_w_out),
    )
    ln_mix = ((l0_ln_mix_g, l0_ln_mix_b), (l1_ln_mix_g, l1_ln_mix_b))
    moe_params = ((l0_router_w, l0_router_b, l0_w1, l0_b1, l0_w2, l0_b2),
                  (l1_router_w, l1_router_b, l1_w1, l1_b1, l1_w2, l1_b2))
    ln_ffn = ((l0_ln_ffn_g, l0_ln_ffn_b), (l1_ln_ffn_g, l1_ln_ffn_b))
    for i in range(DEPTH):
        if i % 2 == 0:
            m = pool_ssd_mixer(x, *mix_params[i])
        else:
            m = shortconv_sb_mixer(x, *mix_params[i])
        x = layer_norm(DEEPNORM_ALPHA * x + m, *ln_mix[i])
        x = layer_norm(DEEPNORM_ALPHA * x + moe_ffn(x, *moe_params[i]), *ln_ffn[i])
    return x
```

```python
import functools

import jax
import jax.numpy as jnp
from jax import lax
from jax.experimental import pallas as pl
from jax.experimental.pallas import tpu as pltpu

F32 = jnp.float32
BF16 = jnp.bfloat16

D_MODEL = 4096
DEPTH = 2
DEEPNORM_ALPHA = (2.0 * DEPTH) ** 0.25
LN_EPS = 1e-5
RMS_EPS = 1e-5

POOL_WINDOWS = (2, 4, 8, 16)
POOL_GROUPS = 4
D_POOL = D_MODEL // 2
POOL_GROUP_DIM = D_POOL // POOL_GROUPS

SSD_HEADDIM = 64
D_SSD = (3 * D_MODEL) // 2
SSD_HEADS = D_SSD // SSD_HEADDIM
SSD_GROUPS = 8
SSD_STATE = 128
SSD_CONV = 4
SSD_GROUP_HEADS = SSD_HEADS // SSD_GROUPS
SSD_GROUP_DIM = D_SSD // SSD_GROUPS
SSD_BC = SSD_GROUPS * SSD_STATE

D_CONV = D_MODEL // 2
CONV_WIDTH = 3
SB_HEADS = 16
SB_HEADDIM = 128
D_SB = SB_HEADS * SB_HEADDIM

N_EXPERTS = 32
TOP_K = 4
D_EXPERT = 512
SWIGLU_LIMIT = 7.0
SWIGLU_ALPHA = 1.702

V7X_LANES = 128
V7X_BF16_SUBLANES = 16
V7X_VMEM_LIMIT = 56 * 1024 * 1024
HALO = V7X_BF16_SUBLANES

SSD_CHUNK = 256
POOL_TILE = 512
CONV_TILE = 512
LN_TILE = 256
ROUTER_TILE = 512
SB_TILE = 256
MOE_TILE = 256
COMBINE_TILE = 128


def _params(semantics):
    return pltpu.CompilerParams(dimension_semantics=semantics, vmem_limit_bytes=V7X_VMEM_LIMIT)


def _softplus(x):
    return jnp.maximum(x, 0.0) + jnp.log(1.0 + jnp.exp(-jnp.abs(x)))


def _sigmoid(x):
    return 1.0 / (1.0 + jnp.exp(-x))


def _mm_body(a_ref, w_ref, o_ref, *scratch, nk):
    part = jnp.dot(a_ref[...], w_ref[...], preferred_element_type=F32)
    if nk == 1:
        o_ref[...] = part.astype(o_ref.dtype)
        return
    (acc_ref,) = scratch
    k = pl.program_id(2)

    @pl.when(k == 0)
    def _():
        acc_ref[...] = part

    @pl.when(k > 0)
    def _():
        acc_ref[...] += part

    @pl.when(k == nk - 1)
    def _():
        o_ref[...] = acc_ref[...].astype(o_ref.dtype)


def _matmul(a, w, out_dtype, tm, tn, tk):
    m, kdim = a.shape
    n = w.shape[1]
    tm, tn, tk = min(tm, m), min(tn, n), min(tk, kdim)
    assert m % tm == 0 and n % tn == 0 and kdim % tk == 0
    nk = kdim // tk
    return pl.pallas_call(
        functools.partial(_mm_body, nk=nk),
        grid=(n // tn, m // tm, nk),
        in_specs=[pl.BlockSpec((tm, tk), lambda j, i, k: (i, k)),
                  pl.BlockSpec((tk, tn), lambda j, i, k: (k, j))],
        out_specs=pl.BlockSpec((tm, tn), lambda j, i, k: (i, j)),
        out_shape=jax.ShapeDtypeStruct((m, n), out_dtype),
        scratch_shapes=[pltpu.VMEM((tm, tn), F32)] if nk > 1 else [],
        compiler_params=_params(("parallel", "parallel", "arbitrary")),
        name="matmul",
    )(a, w)


def _layer_norm_rows(v, g, b):
    mu = jnp.mean(v, axis=-1, keepdims=True)
    vc = v - mu
    var = jnp.mean(vc * vc, axis=-1, keepdims=True)
    return vc * lax.rsqrt(var + LN_EPS) * g + b


def _ln_body(x_ref, m_ref, g_ref, b_ref, of_ref, ob_ref):
    v = DEEPNORM_ALPHA * x_ref[...] + m_ref[...].astype(F32)
    y = _layer_norm_rows(v, g_ref[...], b_ref[...])
    of_ref[...] = y
    ob_ref[...] = y.astype(BF16)


def _residual_layer_norm(x, m, g, b):
    t, d = x.shape
    tm = min(LN_TILE, t)
    row = pl.BlockSpec((tm, d), lambda i: (i, 0))
    vec = pl.BlockSpec((1, d), lambda i: (0, 0))
    return pl.pallas_call(
        _ln_body,
        grid=(t // tm,),
        in_specs=[row, row, vec, vec],
        out_specs=[row, row],
        out_shape=[jax.ShapeDtypeStruct((t, d), F32), jax.ShapeDtypeStruct((t, d), BF16)],
        compiler_params=_params(("parallel",)),
        name="residual_layer_norm",
    )(x, m, g.reshape(1, d), b.reshape(1, d))


def _pool_body(u_ref, halo_ref, w_ref, scale_ref, o_ref, ext_ref, *, ts):
    i = pl.program_id(1)
    keep = jnp.where(i > 0, 1.0, 0.0).astype(F32)
    pos = (i * ts + lax.broadcasted_iota(jnp.int32, (ts, 1), 0) + 1).astype(F32)
    for g, win in enumerate(POOL_WINDOWS):
        cols = slice(g * POOL_GROUP_DIM, (g + 1) * POOL_GROUP_DIM)
        cur = u_ref[:, cols].astype(F32)
        ext_ref[0:HALO, :] = halo_ref[:, cols].astype(F32) * keep
        ext_ref[HALO:HALO + ts, :] = cur
        acc = cur
        for back in range(1, win):
            acc = acc + ext_ref[HALO - back:HALO - back + ts, :]
        mean = acc / jnp.minimum(pos, float(win))
        mixed = jnp.dot((mean - cur).astype(BF16), w_ref[g], preferred_element_type=F32)
        o_ref[:, cols] = (mixed * scale_ref[:, cols]).astype(o_ref.dtype)


def _pool_mixer(h, col_block, pool_w, pool_scale, batch, seq):
    ts = min(POOL_TILE, seq)
    nt = seq // ts
    hb = ts // HALO

    def cur_map(b, i):
        return (b * nt + i, col_block)

    def halo_map(b, i):
        return (jnp.maximum((b * nt + i) * hb - 1, 0), col_block)

    return pl.pallas_call(
        functools.partial(_pool_body, ts=ts),
        grid=(batch, nt),
        in_specs=[pl.BlockSpec((ts, D_POOL), cur_map),
                  pl.BlockSpec((HALO, D_POOL), halo_map),
                  pl.BlockSpec((POOL_GROUPS, POOL_GROUP_DIM, POOL_GROUP_DIM), lambda b, i: (0, 0, 0)),
                  pl.BlockSpec((1, D_POOL), lambda b, i: (0, 0))],
        out_specs=pl.BlockSpec((ts, D_POOL), lambda b, i: (b * nt + i, 0)),
        out_shape=jax.ShapeDtypeStruct((batch * seq, D_POOL), BF16),
        scratch_shapes=[pltpu.VMEM((HALO + ts, POOL_GROUP_DIM), F32)],
        compiler_params=_params(("parallel", "parallel")),
        name="pool_mixer",
    )(h, h, pool_w, pool_scale.reshape(1, D_POOL))


def _conv_silu(cur_ref, halo_ref, w_ref, b_ref, ext_ref, keep, width, ln):
    ext_ref[0:HALO, 0:width] = halo_ref[...].astype(F32) * keep
    ext_ref[HALO:HALO + ln, 0:width] = cur_ref[...].astype(F32)
    acc = b_ref[...]
    for tap in range(SSD_CONV):
        off = HALO - (SSD_CONV - 1) + tap
        acc = acc + w_ref[tap:tap + 1, :] * ext_ref[off:off + ln, 0:width]
    return acc * _sigmoid(acc)


def _ssd_body(z_ref, x_ref, xh_ref, b_ref, bh_ref, c_ref, ch_ref, dt_ref,
              cwx_ref, cbx_ref, cwb_ref, cbb_ref, cwc_ref, cbc_ref,
              dtb_ref, alog_ref, dskip_ref, nw_ref, o_ref, ext_ref, state_ref, *, ln):
    c = pl.program_id(2)
    keep = jnp.where(c > 0, 1.0, 0.0).astype(F32)

    @pl.when(c == 0)
    def _():
        state_ref[...] = jnp.zeros_like(state_ref)

    hp = lax.Precision.HIGHEST
    gh, p, gd = SSD_GROUP_HEADS, SSD_HEADDIM, SSD_GROUP_DIM

    xs = _conv_silu(x_ref, xh_ref, cwx_ref, cbx_ref, ext_ref, keep, gd, ln)
    bm = _conv_silu(b_ref, bh_ref, cwb_ref, cbb_ref, ext_ref, keep, SSD_STATE, ln)
    cm = _conv_silu(c_ref, ch_ref, cwc_ref, cbc_ref, ext_ref, keep, SSD_STATE, ln)

    dt = _softplus(dt_ref[...] + dtb_ref[...])
    da = dt * (-jnp.exp(alog_ref[...]))

    row = lax.broadcasted_iota(jnp.int32, (ln, ln), 0)
    col = lax.broadcasted_iota(jnp.int32, (ln, ln), 1)
    causal = row >= col
    tri = jnp.where(causal, 1.0, 0.0).astype(F32)
    acum = jnp.dot(tri, da, precision=hp, preferred_element_type=F32)
    acum_t = acum.T

    eh = lax.broadcasted_iota(jnp.int32, (V7X_LANES, gd), 0)
    ec = lax.broadcasted_iota(jnp.int32, (V7X_LANES, gd), 1)
    expand = jnp.where((ec >= eh * p) & (ec < (eh + 1) * p), 1.0, 0.0).astype(F32)
    dt_x = jnp.dot(dt, expand, precision=hp, preferred_element_type=F32)
    acum_x = jnp.dot(acum, expand, precision=hp, preferred_element_type=F32)
    last_x = acum_x[ln - 1:ln, :]

    xdt = xs * dt_x
    xdt_b = xdt.astype(BF16)
    bm_b = bm.astype(BF16)
    cm_b = cm.astype(BF16)

    cb = lax.dot_general(cm_b, bm_b, (((1,), (1,)), ((), ())), preferred_element_type=F32)

    def head_diag(e, x_pair):
        seg = acum[:, e:e + 1] - acum_t[e:e + 1, :]
        decay = jnp.where(causal, jnp.exp(jnp.minimum(seg, 0.0)), 0.0)
        return jnp.dot((cb * decay).astype(BF16), x_pair, preferred_element_type=F32)

    first_half = lax.broadcasted_iota(jnp.int32, (ln, 2 * p), 1) < p
    parts = []
    for pair in range(gh // 2):
        x_pair = xdt_b[:, pair * 2 * p:(pair + 1) * 2 * p]
        parts.append(jnp.where(first_half, head_diag(2 * pair, x_pair), head_diag(2 * pair + 1, x_pair)))
    y_diag = jnp.concatenate(parts, axis=1)

    state = state_ref[...]
    y_off = jnp.dot(cm_b, state.astype(BF16), preferred_element_type=F32) * jnp.exp(acum_x)
    to_end = jnp.exp(last_x - acum_x)
    upd = jnp.dot(bm.T.astype(BF16), (xdt * to_end).astype(BF16), preferred_element_type=F32)
    state_ref[...] = state * jnp.exp(last_x) + upd

    y = y_diag + y_off + dskip_ref[...] * xs
    zf = z_ref[...].astype(F32)
    y = y * (zf * _sigmoid(zf))
    y = y * lax.rsqrt(jnp.mean(y * y, axis=-1, keepdims=True) + RMS_EPS)
    o_ref[...] = (y * nw_ref[...]).astype(o_ref.dtype)


def _ssd_mixer(h, dt_raw, conv_w, conv_b, dt_bias, a_log, d_skip, norm_w, batch, seq,
               z_block, x_block, b_block, c_block):
    ln = min(SSD_CHUNK, seq)
    nc = seq // ln
    hb = ln // HALO
    gd, gh = SSD_GROUP_DIM, SSD_GROUP_HEADS
    ngb = D_SSD // SSD_STATE

    def cur(base):
        return lambda b, g, c: (b * nc + c, base + g)

    def halo(base):
        return lambda b, g, c: (jnp.maximum((b * nc + c) * hb - 1, 0), base + g)

    grp = lambda b, g, c: (g, 0, 0)
    in_specs = [
        pl.BlockSpec((ln, gd), cur(z_block)),
        pl.BlockSpec((ln, gd), cur(x_block)), pl.BlockSpec((HALO, gd), halo(x_block)),
        pl.BlockSpec((ln, SSD_STATE), cur(b_block)), pl.BlockSpec((HALO, SSD_STATE), halo(b_block)),
        pl.BlockSpec((ln, SSD_STATE), cur(c_block)), pl.BlockSpec((HALO, SSD_STATE), halo(c_block)),
        pl.BlockSpec((None, ln, V7X_LANES), lambda b, g, c: (g, b * nc + c, 0)),
        pl.BlockSpec((SSD_CONV, gd), lambda b, g, c: (0, g)), pl.BlockSpec((1, gd), lambda b, g, c: (0, g)),
        pl.BlockSpec((SSD_CONV, SSD_STATE), lambda b, g, c: (0, ngb + g)),
        pl.BlockSpec((1, SSD_STATE), lambda b, g, c: (0, ngb + g)),
        pl.BlockSpec((SSD_CONV, SSD_STATE), lambda b, g, c: (0, ngb + SSD_GROUPS + g)),
        pl.BlockSpec((1, SSD_STATE), lambda b, g, c: (0, ngb + SSD_GROUPS + g)),
        pl.BlockSpec((None, 1, V7X_LANES), grp), pl.BlockSpec((None, 1, V7X_LANES), grp),
        pl.BlockSpec((1, gd), lambda b, g, c: (0, g)), pl.BlockSpec((1, gd), lambda b, g, c: (0, g)),
    ]
    conv_b2 = conv_b.reshape(1, -1)
    d_skip_x = jnp.repeat(d_skip.astype(F32), SSD_HEADDIM).reshape(1, D_SSD)

    def per_head(v):
        return jnp.pad(v.astype(F32).reshape(SSD_GROUPS, 1, gh), ((0, 0), (0, 0), (0, V7X_LANES - gh)))
    return pl.pallas_call(
        functools.partial(_ssd_body, ln=ln),
        grid=(batch, SSD_GROUPS, nc),
        in_specs=in_specs,
        out_specs=pl.BlockSpec((ln, gd), lambda b, g, c: (b * nc + c, g)),
        out_shape=jax.ShapeDtypeStruct((batch * seq, D_SSD), BF16),
        scratch_shapes=[pltpu.VMEM((HALO + ln, gd), F32), pltpu.VMEM((SSD_STATE, gd), F32)],
        compiler_params=_params(("parallel", "parallel", "arbitrary")),
        name="ssd_mixer",
    )(h, h, h, h, h, h, h, dt_raw,
      conv_w, conv_b2, conv_w, conv_b2, conv_w, conv_b2,
      per_head(dt_bias), per_head(a_log), d_skip_x, norm_w.reshape(1, D_SSD))


def _gconv_body(bg_ref, cg_ref, cgh_ref, xi_ref, xih_ref, w_ref, o_ref, ext_ref, *, ts):
    i = pl.program_id(1)
    keep = jnp.where(i > 0, 1.0, 0.0).astype(F32)
    ext_ref[0:HALO, :] = cgh_ref[...].astype(F32) * xih_ref[...].astype(F32) * keep
    ext_ref[HALO:HALO + ts, :] = cg_ref[...].astype(F32) * xi_ref[...].astype(F32)
    acc = jnp.zeros((ts, D_CONV), F32)
    for tap in range(CONV_WIDTH):
        off = HALO - (CONV_WIDTH - 1) + tap
        acc = acc + w_ref[tap:tap + 1, :] * ext_ref[off:off + ts, :]
    o_ref[...] = (bg_ref[...].astype(F32) * acc).astype(o_ref.dtype)


def _gated_conv(h, conv_w, batch, seq):
    ts = min(CONV_TILE, seq)
    nt = seq // ts
    hb = ts // HALO

    def cur(blk):
        return lambda b, i: (b * nt + i, blk)

    def halo(blk):
        return lambda b, i: (jnp.maximum((b * nt + i) * hb - 1, 0), blk)

    return pl.pallas_call(
        functools.partial(_gconv_body, ts=ts),
        grid=(batch, nt),
        in_specs=[pl.BlockSpec((ts, D_CONV), cur(0)),
                  pl.BlockSpec((ts, D_CONV), cur(1)), pl.BlockSpec((HALO, D_CONV), halo(1)),
                  pl.BlockSpec((ts, D_CONV), cur(2)), pl.BlockSpec((HALO, D_CONV), halo(2)),
                  pl.BlockSpec((CONV_WIDTH, D_CONV), lambda b, i: (0, 0))],
        out_specs=pl.BlockSpec((ts, D_CONV), lambda b, i: (b * nt + i, 0)),
        out_shape=jax.ShapeDtypeStruct((batch * seq, D_CONV), BF16),
        scratch_shapes=[pltpu.VMEM((HALO + ts, D_CONV), F32)],
        compiler_params=_params(("parallel", "parallel")),
        name="gated_conv",
    )(h, h, h, h, h, conv_w)


def _sb_block(q, k_ref, v_ref, start, tk, carry, acc, upper, mask):
    kt = k_ref[pl.ds(start, tk), :]
    vt = v_ref[pl.ds(start, tk), :]
    z = lax.dot_general(q, kt, (((1,), (1,)), ((), ())), preferred_element_type=F32)
    sp = _softplus(z)
    log_stay = -sp if mask is None else jnp.where(mask, -sp, 0.0)
    hi = log_stay.astype(BF16)
    lo = (log_stay - hi.astype(F32)).astype(BF16)
    after = (jnp.dot(hi, upper, preferred_element_type=F32)
             + jnp.dot(lo, upper, preferred_element_type=F32)) + carry
    w = jnp.exp((z - sp) + after)
    if mask is not None:
        w = jnp.where(mask, w, 0.0)
    acc = acc + jnp.dot(w.astype(BF16), vt, preferred_element_type=F32)
    carry = carry + jnp.sum(log_stay, axis=1, keepdims=True)
    return carry, acc


def _sb_body(q_ref, k_ref, v_ref, o_ref, *, tq):
    qi = pl.program_id(2)
    q = (q_ref[...].astype(F32) * (SB_HEADDIM ** -0.5)).astype(BF16)
    row = lax.broadcasted_iota(jnp.int32, (tq, tq), 0)
    col = lax.broadcasted_iota(jnp.int32, (tq, tq), 1)
    upper = jnp.where(row > col, 1.0, 0.0).astype(BF16)
    diag_mask = col < row
    carry = jnp.zeros((tq, 1), F32)
    acc = jnp.zeros((tq, SB_HEADDIM), F32)
    carry, acc = _sb_block(q, k_ref, v_ref, pl.multiple_of(qi * tq, tq), tq, carry, acc, upper, diag_mask)

    def step(j, state):
        start = pl.multiple_of((qi - 1 - j) * tq, tq)
        return _sb_block(q, k_ref, v_ref, start, tq, state[0], state[1], upper, None)

    carry, acc = lax.fori_loop(0, qi, step, (carry, acc))
    o_ref[...] = acc.astype(o_ref.dtype)


def _sb_attention(h, q_block, k_block, v_block, batch, seq):
    tq = min(SB_TILE, seq)
    nq = seq // tq
    return pl.pallas_call(
        functools.partial(_sb_body, tq=tq),
        grid=(batch, SB_HEADS, nq),
        in_specs=[pl.BlockSpec((tq, SB_HEADDIM), lambda b, hd, i: (b * nq + i, q_block + hd)),
                  pl.BlockSpec((seq, SB_HEADDIM), lambda b, hd, i: (b, k_block + hd)),
                  pl.BlockSpec((seq, SB_HEADDIM), lambda b, hd, i: (b, v_block + hd))],
        out_specs=pl.BlockSpec((tq, SB_HEADDIM), lambda b, hd, i: (b * nq + i, hd)),
        out_shape=jax.ShapeDtypeStruct((batch * seq, D_SB), BF16),
        compiler_params=_params(("parallel", "parallel", "arbitrary")),
        name="stick_breaking_attention",
    )(h, h, h)


def _router_body(x_ref, wt_ref, b_ref, idx_ref, wgt_ref):
    logits = lax.dot_general(wt_ref[...], x_ref[...], (((1,), (1,)), ((), ())),
                             precision=lax.Precision.HIGHEST, preferred_element_type=F32) + b_ref[...]
    iota = lax.broadcasted_iota(jnp.int32, logits.shape, 0).astype(F32)
    vals, idxs = [], []
    for _ in range(TOP_K):
        mx = jnp.max(logits, axis=0, keepdims=True)
        ix = jnp.min(jnp.where(logits == mx, iota, float(N_EXPERTS)), axis=0, keepdims=True)
        vals.append(mx)
        idxs.append(ix)
        logits = jnp.where(iota == ix, -jnp.inf, logits)
    exps = [jnp.exp(v - vals[0]) for v in vals]
    denom = exps[0] + exps[1] + exps[2] + exps[3]
    for k in range(TOP_K):
        idx_ref[k:k + 1, :] = idxs[k].astype(jnp.int32)
        wgt_ref[k:k + 1, :] = exps[k] / denom


def _router(x, router_w, router_b):
    t, d = x.shape
    tm = min(ROUTER_TILE, t)
    out = pl.BlockSpec((TOP_K, tm), lambda i: (0, i))
    return pl.pallas_call(
        _router_body,
        grid=(t // tm,),
        in_specs=[pl.BlockSpec((tm, d), lambda i: (i, 0)),
                  pl.BlockSpec((N_EXPERTS, d), lambda i: (0, 0)),
                  pl.BlockSpec((N_EXPERTS, 1), lambda i: (0, 0))],
        out_specs=[out, out],
        out_shape=[jax.ShapeDtypeStruct((TOP_K, t), jnp.int32), jax.ShapeDtypeStruct((TOP_K, t), F32)],
        compiler_params=_params(("parallel",)),
        name="moe_router",
    )(x, router_w.T, router_b.reshape(N_EXPERTS, 1))


def _routing_tables(idx, wgt, tm):
    t = idx.shape[1]
    pairs = TOP_K * t
    rows = pairs + N_EXPERTS * tm
    ntiles = rows // tm
    e_flat = idx.reshape(pairs)
    onehot = (e_flat[:, None] == jnp.arange(N_EXPERTS, dtype=jnp.int32)[None, :]).astype(jnp.int32)
    csum = jnp.cumsum(onehot, axis=0)
    rank = jnp.sum((csum - onehot) * onehot, axis=1)
    counts = csum[-1]
    padded = ((counts + tm - 1) // tm) * tm
    gend = jnp.cumsum(padded)
    gstart = gend - padded
    pos = (gstart[e_flat] + rank).astype(jnp.int32)
    token = (jnp.arange(pairs, dtype=jnp.int32) % t)
    tok_of_row = jnp.zeros((rows,), jnp.int32).at[pos].set(token)
    gate_of_row = jnp.zeros((rows,), F32).at[pos].set(wgt.reshape(pairs))
    tile_start = jnp.arange(ntiles, dtype=jnp.int32) * tm
    tile_expert = jnp.minimum(jnp.searchsorted(gend, tile_start, side="right"), N_EXPERTS - 1).astype(jnp.int32)
    tile_valid = (tile_start < gend[-1]).astype(jnp.int32)
    pos_tok = pos.reshape(TOP_K, t).T.reshape(pairs)
    return tok_of_row, gate_of_row.reshape(rows, 1), pos_tok, tile_expert, tile_valid


def _experts_body(te_ref, tv_ref, tok_ref, x_hbm, w1_ref, b1_ref, w2_ref, b2_ref, g_ref, o_ref,
                  xbuf, sem, *, tm):
    i = pl.program_id(0)
    nt = pl.num_programs(0)

    def row_copy(tok, slot, r):
        return pltpu.make_async_copy(x_hbm.at[pl.ds(tok, 1), :], xbuf.at[slot, pl.ds(r, 1), :], sem.at[slot])

    def issue(tile, slot):
        base = tile * tm

        def body(r, carry):
            row_copy(tok_ref[base + r], slot, r).start()
            return carry

        lax.fori_loop(0, tm, body, 0, unroll=8)

    @pl.when(i == 0)
    def _():
        issue(0, 0)

    @pl.when(i + 1 < nt)
    def _():
        issue(i + 1, (i + 1) % 2)

    slot = i % 2
    pltpu.make_async_copy(x_hbm.at[pl.ds(0, tm), :], xbuf.at[slot], sem.at[slot]).wait()

    @pl.when(tv_ref[i] == 1)
    def _():
        x = xbuf[slot].astype(BF16)
        hh = jnp.dot(x, w1_ref[...], preferred_element_type=F32) + b1_ref[...]
        glu = jnp.minimum(hh[:, :D_EXPERT], SWIGLU_LIMIT)
        lin = jnp.clip(hh[:, D_EXPERT:], -SWIGLU_LIMIT, SWIGLU_LIMIT)
        act = glu * _sigmoid(SWIGLU_ALPHA * glu) * (lin + 1.0)
        out = jnp.dot(act.astype(BF16), w2_ref[...], preferred_element_type=F32) + b2_ref[...]
        o_ref[...] = out * g_ref[...]

    @pl.when(tv_ref[i] == 0)
    def _():
        o_ref[...] = jnp.zeros_like(o_ref)


def _experts(x, w1, b1, w2, b2, tok_of_row, gate_of_row, tile_expert, tile_valid, tm):
    t, d = x.shape
    rows = gate_of_row.shape[0]
    ntiles = rows // tm
    grid_spec = pltpu.PrefetchScalarGridSpec(
        num_scalar_prefetch=3,
        grid=(ntiles,),
        in_specs=[pl.BlockSpec(memory_space=pl.ANY),
                  pl.BlockSpec((None, d, 2 * D_EXPERT), lambda i, te, tv, tok: (te[i], 0, 0)),
                  pl.BlockSpec((None, 1, 2 * D_EXPERT), lambda i, te, tv, tok: (te[i], 0, 0)),
                  pl.BlockSpec((None, D_EXPERT, d), lambda i, te, tv, tok: (te[i], 0, 0)),
                  pl.BlockSpec((None, 1, d), lambda i, te, tv, tok: (te[i], 0, 0)),
                  pl.BlockSpec((tm, 1), lambda i, te, tv, tok: (i, 0))],
        out_specs=pl.BlockSpec((tm, d), lambda i, te, tv, tok: (i, 0)),
        scratch_shapes=[pltpu.VMEM((2, tm, d), F32), pltpu.SemaphoreType.DMA((2,))],
    )
    return pl.pallas_call(
        functools.partial(_experts_body, tm=tm),
        grid_spec=grid_spec,
        out_shape=jax.ShapeDtypeStruct((rows, d), F32),
        compiler_params=_params(("arbitrary",)),
        name="moe_experts",
    )(tile_expert, tile_valid, tok_of_row, x, w1, b1.reshape(N_EXPERTS, 1, -1), w2,
      b2.reshape(N_EXPERTS, 1, -1), gate_of_row)


def _combine_body(pos_ref, y_hbm, x_ref, g_ref, b_ref, of_ref, ob_ref, ybuf, sem, *, tm):
    i = pl.program_id(0)
    nt = pl.num_programs(0)

    def row_copy(src, slot, k, r):
        return pltpu.make_async_copy(y_hbm.at[pl.ds(src, 1), :], ybuf.at[slot, k, pl.ds(r, 1), :], sem.at[slot])

    def issue(tile, slot):
        base = tile * tm * TOP_K

        def body(r, carry):
            for k in range(TOP_K):
                row_copy(pos_ref[base + r * TOP_K + k], slot, k, r).start()
            return carry

        lax.fori_loop(0, tm, body, 0, unroll=4)

    @pl.when(i == 0)
    def _():
        issue(0, 0)

    @pl.when(i + 1 < nt)
    def _():
        issue(i + 1, (i + 1) % 2)

    slot = i % 2
    for k in range(TOP_K):
        pltpu.make_async_copy(y_hbm.at[pl.ds(0, tm), :], ybuf.at[slot, k], sem.at[slot]).wait()
    moe = (ybuf[slot, 0] + ybuf[slot, 1]) + (ybuf[slot, 2] + ybuf[slot, 3])
    y = _layer_norm_rows(DEEPNORM_ALPHA * x_ref[...] + moe, g_ref[...], b_ref[...])
    of_ref[...] = y
    ob_ref[...] = y.astype(BF16)


def _combine_layer_norm(x, y_rows, pos_tok, g, b, tm):
    t, d = x.shape
    row = lambda i, pos: (i, 0)
    vec = lambda i, pos: (0, 0)
    grid_spec = pltpu.PrefetchScalarGridSpec(
        num_scalar_prefetch=1,
        grid=(t // tm,),
        in_specs=[pl.BlockSpec(memory_space=pl.ANY),
                  pl.BlockSpec((tm, d), row), pl.BlockSpec((1, d), vec), pl.BlockSpec((1, d), vec)],
        out_specs=[pl.BlockSpec((tm, d), row), pl.BlockSpec((tm, d), row)],
        scratch_shapes=[pltpu.VMEM((2, TOP_K, tm, d), F32), pltpu.SemaphoreType.DMA((2,))],
    )
    return pl.pallas_call(
        functools.partial(_combine_body, tm=tm),
        grid_spec=grid_spec,
        out_shape=[jax.ShapeDtypeStruct((t, d), F32), jax.ShapeDtypeStruct((t, d), BF16)],
        compiler_params=_params(("arbitrary",)),
        name="moe_combine_layer_norm",
    )(pos_tok, y_rows, x, g.reshape(1, d), b.reshape(1, d))


def _moe_block(x_f32, router_w, router_b, w1, b1, w2, b2, g, b):
    t = x_f32.shape[0]
    tm = min(MOE_TILE, t)
    idx, wgt = _router(x_f32, router_w, router_b)
    tok_of_row, gate_of_row, pos_tok, tile_expert, tile_valid = _routing_tables(idx, wgt, tm)
    y_rows = _experts(x_f32, w1.astype(BF16), b1, w2.astype(BF16), b2,
                      tok_of_row, gate_of_row, tile_expert, tile_valid, tm)
    return _combine_layer_norm(x_f32, y_rows, pos_tok, g, b, min(COMBINE_TILE, t))


def _layer0_mixer(x_f32, x_b16, batch, seq, w_in, conv_w, conv_b, dt_bias, a_log, d_skip, norm_w,
                  pool_w, pool_scale, w_out, ln_g, ln_b):
    o_u, o_z, o_xbc, o_dt = 0, D_POOL, D_POOL + D_SSD, D_POOL + D_SSD + D_SSD + 2 * SSD_BC
    w_main = jnp.concatenate([w_in[:, o_z:o_xbc], w_in[:, o_xbc:o_dt], w_in[:, o_u:o_z]], axis=1).astype(BF16)
    w_dt = jnp.pad(w_in[:, o_dt:], ((0, 0), (0, V7X_LANES - SSD_HEADS))).astype(BF16)
    h = _matmul(x_b16, w_main, BF16, 1024, 1024, D_MODEL)
    dt_raw = _matmul(x_b16, w_dt, F32, 1024, V7X_LANES, D_MODEL)[:, :SSD_HEADS]
    t = batch * seq
    dt_raw = dt_raw.reshape(t, SSD_GROUPS, SSD_GROUP_HEADS).transpose(1, 0, 2)
    dt_raw = jnp.pad(dt_raw, ((0, 0), (0, 0), (0, V7X_LANES - SSD_GROUP_HEADS)))
    y_ssd = _ssd_mixer(h, dt_raw, conv_w, conv_b, dt_bias, a_log, d_skip, norm_w, batch, seq,
                       z_block=0, x_block=D_SSD // SSD_GROUP_DIM,
                       b_block=2 * D_SSD // SSD_STATE, c_block=(2 * D_SSD + SSD_BC) // SSD_STATE)
    y_pool = _pool_mixer(h, (2 * D_SSD + 2 * SSD_BC) // D_POOL, pool_w.astype(BF16), pool_scale, batch, seq)
    y = jnp.concatenate([y_pool, y_ssd], axis=1)
    m = _matmul(y, w_out.astype(BF16), F32, 1024, 1024, D_MODEL)
    return _residual_layer_norm(x_f32, m, ln_g, ln_b)


def _layer1_mixer(x_f32, x_b16, batch, seq, w_in, conv_w, w_out, ln_g, ln_b):
    h = _matmul(x_b16, w_in.astype(BF16), BF16, 1024, 1024, D_MODEL)
    y_conv = _gated_conv(h, conv_w, batch, seq)
    qb = 3 * D_CONV // SB_HEADDIM
    y_sb = _sb_attention(h, qb, qb + SB_HEADS, qb + 2 * SB_HEADS, batch, seq)
    y = jnp.concatenate([y_conv, y_sb], axis=1)
    m = _matmul(y, w_out.astype(BF16), F32, 1024, 1024, D_MODEL)
    return _residual_layer_norm(x_f32, m, ln_g, ln_b)


def kernel(x, l0_w_in, l0_conv_w, l0_conv_b, l0_dt_bias, l0_a_log, l0_d_skip, l0_ssm_norm_w, l0_pool_w, l0_pool_scale, l0_w_out, l0_ln_mix_g, l0_ln_mix_b, l0_router_w, l0_router_b, l0_w1, l0_b1, l0_w2, l0_b2, l0_ln_ffn_g, l0_ln_ffn_b, l1_w_in, l1_conv_w, l1_w_out, l1_ln_mix_g, l1_ln_mix_b, l1_router_w, l1_router_b, l1_w1, l1_b1, l1_w2, l1_b2, l1_ln_ffn_g, l1_ln_ffn_b):
    batch, seq, d = x.shape
    xf = x.reshape(batch * seq, d)
    xb = xf.astype(BF16)
    xf, xb = _layer0_mixer(xf, xb, batch, seq, l0_w_in, l0_conv_w, l0_conv_b, l0_dt_bias, l0_a_log, l0_d_skip,
                           l0_ssm_norm_w, l0_pool_w, l0_pool_scale, l0_w_out, l0_ln_mix_g, l0_ln_mix_b)
    xf, xb = _moe_block(xf, l0_router_w, l0_router_b, l0_w1, l0_b1, l0_w2, l0_b2, l0_ln_ffn_g, l0_ln_ffn_b)
    xf, xb = _layer1_mixer(xf, xb, batch, seq, l1_w_in, l1_conv_w, l1_w_out, l1_ln_mix_g, l1_ln_mix_b)
    xf, xb = _moe_block(xf, l1_router_w, l1_router_b, l1_w1, l1_b1, l1_w2, l1_b2, l1_ln_ffn_g, l1_ln_ffn_b)
    return xf.reshape(batch, seq, d)
```

```python
import functools

import jax
import jax.numpy as jnp
from jax import lax
from jax.experimental import pallas as pl
from jax.experimental.pallas import tpu as pltpu

F32 = jnp.float32
BF16 = jnp.bfloat16

D_MODEL = 4096
DEPTH = 2
DEEPNORM_ALPHA = (2.0 * DEPTH) ** 0.25
LN_EPS = 1e-5
RMS_EPS = 1e-5

POOL_WINDOWS = (2, 4, 8, 16)
POOL_GROUPS = 4
D_POOL = D_MODEL // 2
POOL_GROUP_DIM = D_POOL // POOL_GROUPS

SSD_HEADDIM = 64
D_SSD = (3 * D_MODEL) // 2
SSD_HEADS = D_SSD // SSD_HEADDIM
SSD_GROUPS = 8
SSD_STATE = 128
SSD_CONV = 4
SSD_GROUP_HEADS = SSD_HEADS // SSD_GROUPS
SSD_GROUP_DIM = D_SSD // SSD_GROUPS
SSD_BC = SSD_GROUPS * SSD_STATE

D_CONV = D_MODEL // 2
CONV_WIDTH = 3
SB_HEADS = 16
SB_HEADDIM = 128
D_SB = SB_HEADS * SB_HEADDIM

N_EXPERTS = 32
TOP_K = 4
D_EXPERT = 512
SWIGLU_LIMIT = 7.0
SWIGLU_ALPHA = 1.702

V7X_LANES = 128
V7X_BF16_SUBLANES = 16
V7X_VMEM_LIMIT = 56 * 1024 * 1024
HALO = V7X_BF16_SUBLANES

SSD_CHUNK = 256
POOL_TILE = 512
CONV_TILE = 512
LN_TILE = 256
ROUTER_TILE = 512
SB_TILE = 256
SB_UNDERFLOW_LOG = -110.0
MOE_TILE = 256
MOE_ISSUE_CHUNKS = 8
COMBINE_TILE = 128


def _params(semantics):
    return pltpu.CompilerParams(dimension_semantics=semantics, vmem_limit_bytes=V7X_VMEM_LIMIT)


def _softplus(x):
    return jnp.maximum(x, 0.0) + jnp.log(1.0 + jnp.exp(-jnp.abs(x)))


def _sigmoid(x):
    return 1.0 / (1.0 + jnp.exp(-x))


def _mm_body(a_ref, w_ref, o_ref, *scratch, nk):
    part = jnp.dot(a_ref[...], w_ref[...], preferred_element_type=F32)
    if nk == 1:
        o_ref[...] = part.astype(o_ref.dtype)
        return
    (acc_ref,) = scratch
    k = pl.program_id(2)

    @pl.when(k == 0)
    def _():
        acc_ref[...] = part

    @pl.when(k > 0)
    def _():
        acc_ref[...] += part

    @pl.when(k == nk - 1)
    def _():
        o_ref[...] = acc_ref[...].astype(o_ref.dtype)


def _matmul(a, w, out_dtype, tm, tn, tk):
    m, kdim = a.shape
    n = w.shape[1]
    tm, tn, tk = min(tm, m), min(tn, n), min(tk, kdim)
    assert m % tm == 0 and n % tn == 0 and kdim % tk == 0
    nk = kdim // tk
    return pl.pallas_call(
        functools.partial(_mm_body, nk=nk),
        grid=(n // tn, m // tm, nk),
        in_specs=[pl.BlockSpec((tm, tk), lambda j, i, k: (i, k)),
                  pl.BlockSpec((tk, tn), lambda j, i, k: (k, j))],
        out_specs=pl.BlockSpec((tm, tn), lambda j, i, k: (i, j)),
        out_shape=jax.ShapeDtypeStruct((m, n), out_dtype),
        scratch_shapes=[pltpu.VMEM((tm, tn), F32)] if nk > 1 else [],
        compiler_params=_params(("parallel", "parallel", "arbitrary")),
        name="matmul",
    )(a, w)


def _layer_norm_rows(v, g, b):
    mu = jnp.mean(v, axis=-1, keepdims=True)
    vc = v - mu
    var = jnp.mean(vc * vc, axis=-1, keepdims=True)
    return vc * lax.rsqrt(var + LN_EPS) * g + b


def _bf16_bits(v):
    return lax.bitcast_convert_type(v.astype(BF16).astype(F32), jnp.uint32)


def _pack_halves(v):
    n = v.shape[1] // 2
    return _bf16_bits(v[:, n:]) | (_bf16_bits(v[:, :n]) >> 16)


def _unpack_halves(words):
    low = lax.bitcast_convert_type(words << 16, F32)
    high = lax.bitcast_convert_type(words & jnp.uint32(0xFFFF0000), F32)
    return low, high


def _ln_body(x_ref, m_ref, g_ref, b_ref, of_ref, op_ref):
    v = DEEPNORM_ALPHA * x_ref[...] + m_ref[...].astype(F32)
    y = _layer_norm_rows(v, g_ref[...], b_ref[...])
    of_ref[...] = y
    op_ref[...] = _pack_halves(y)


def _residual_layer_norm(x, m, g, b):
    t, d = x.shape
    tm = min(LN_TILE, t)
    row = pl.BlockSpec((tm, d), lambda i: (i, 0))
    half = pl.BlockSpec((tm, d // 2), lambda i: (i, 0))
    vec = pl.BlockSpec((1, d), lambda i: (0, 0))
    return pl.pallas_call(
        _ln_body,
        grid=(t // tm,),
        in_specs=[row, row, vec, vec],
        out_specs=[row, half],
        out_shape=[jax.ShapeDtypeStruct((t, d), F32), jax.ShapeDtypeStruct((t, d // 2), jnp.uint32)],
        compiler_params=_params(("parallel",)),
        name="residual_layer_norm",
    )(x, m, g.reshape(1, d), b.reshape(1, d))


def _pool_body(u_ref, halo_ref, w_ref, scale_ref, o_ref, ext_ref, *, ts):
    i = pl.program_id(1)
    keep = jnp.where(i > 0, 1.0, 0.0).astype(F32)
    pos = (i * ts + lax.broadcasted_iota(jnp.int32, (ts, 1), 0) + 1).astype(F32)
    for g, win in enumerate(POOL_WINDOWS):
        cols = slice(g * POOL_GROUP_DIM, (g + 1) * POOL_GROUP_DIM)
        cur = u_ref[:, cols].astype(F32)
        ext_ref[0:HALO, :] = halo_ref[:, cols].astype(F32) * keep
        ext_ref[HALO:HALO + ts, :] = cur
        acc = cur
        for back in range(1, win):
            acc = acc + ext_ref[HALO - back:HALO - back + ts, :]
        mean = acc / jnp.minimum(pos, float(win))
        mixed = jnp.dot((mean - cur).astype(BF16), w_ref[g], preferred_element_type=F32)
        o_ref[:, cols] = (mixed * scale_ref[:, cols]).astype(o_ref.dtype)


def _pool_mixer(h, col_block, pool_w, pool_scale, batch, seq):
    ts = min(POOL_TILE, seq)
    nt = seq // ts
    hb = ts // HALO

    def cur_map(b, i):
        return (b * nt + i, col_block)

    def halo_map(b, i):
        return (jnp.maximum((b * nt + i) * hb - 1, 0), col_block)

    return pl.pallas_call(
        functools.partial(_pool_body, ts=ts),
        grid=(batch, nt),
        in_specs=[pl.BlockSpec((ts, D_POOL), cur_map),
                  pl.BlockSpec((HALO, D_POOL), halo_map),
                  pl.BlockSpec((POOL_GROUPS, POOL_GROUP_DIM, POOL_GROUP_DIM), lambda b, i: (0, 0, 0)),
                  pl.BlockSpec((1, D_POOL), lambda b, i: (0, 0))],
        out_specs=pl.BlockSpec((ts, D_POOL), lambda b, i: (b * nt + i, 0)),
        out_shape=jax.ShapeDtypeStruct((batch * seq, D_POOL), BF16),
        scratch_shapes=[pltpu.VMEM((HALO + ts, POOL_GROUP_DIM), F32)],
        compiler_params=_params(("parallel", "parallel")),
        name="pool_mixer",
    )(h, h, pool_w, pool_scale.reshape(1, D_POOL))


def _conv_silu(cur_ref, halo_ref, w_ref, b_ref, ext_ref, keep, width, ln):
    ext_ref[0:HALO, 0:width] = halo_ref[...].astype(F32) * keep
    ext_ref[HALO:HALO + ln, 0:width] = cur_ref[...].astype(F32)
    acc = b_ref[...]
    for tap in range(SSD_CONV):
        off = HALO - (SSD_CONV - 1) + tap
        acc = acc + w_ref[tap:tap + 1, :] * ext_ref[off:off + ln, 0:width]
    return acc * _sigmoid(acc)


def _ssd_body(z_ref, x_ref, xh_ref, b_ref, bh_ref, c_ref, ch_ref, dt_ref,
              cwx_ref, cbx_ref, cwb_ref, cbb_ref, cwc_ref, cbc_ref,
              dtb_ref, alog_ref, dskip_ref, nw_ref, o_ref, ext_ref, state_ref, *, ln):
    c = pl.program_id(2)
    keep = jnp.where(c > 0, 1.0, 0.0).astype(F32)

    @pl.when(c == 0)
    def _():
        state_ref[...] = jnp.zeros_like(state_ref)

    hp = lax.Precision.HIGHEST
    gh, p, gd = SSD_GROUP_HEADS, SSD_HEADDIM, SSD_GROUP_DIM

    xs = _conv_silu(x_ref, xh_ref, cwx_ref, cbx_ref, ext_ref, keep, gd, ln)
    bm = _conv_silu(b_ref, bh_ref, cwb_ref, cbb_ref, ext_ref, keep, SSD_STATE, ln)
    cm = _conv_silu(c_ref, ch_ref, cwc_ref, cbc_ref, ext_ref, keep, SSD_STATE, ln)

    dt = _softplus(dt_ref[...] + dtb_ref[...])
    da = dt * (-jnp.exp(alog_ref[...]))

    row = lax.broadcasted_iota(jnp.int32, (ln, ln), 0)
    col = lax.broadcasted_iota(jnp.int32, (ln, ln), 1)
    causal = row >= col
    tri = jnp.where(causal, 1.0, 0.0).astype(F32)
    acum = jnp.dot(tri, da, precision=hp, preferred_element_type=F32)
    acum_t = acum.T

    eh = lax.broadcasted_iota(jnp.int32, (V7X_LANES, gd), 0)
    ec = lax.broadcasted_iota(jnp.int32, (V7X_LANES, gd), 1)
    expand = jnp.where((ec >= eh * p) & (ec < (eh + 1) * p), 1.0, 0.0).astype(F32)
    dt_x = jnp.dot(dt, expand, precision=hp, preferred_element_type=F32)
    acum_x = jnp.dot(acum, expand, precision=hp, preferred_element_type=F32)
    last_x = acum_x[ln - 1:ln, :]

    xdt = xs * dt_x
    xdt_b = xdt.astype(BF16)
    bm_b = bm.astype(BF16)
    cm_b = cm.astype(BF16)

    cb = lax.dot_general(cm_b, bm_b, (((1,), (1,)), ((), ())), preferred_element_type=F32)

    def head_diag(e, x_pair):
        seg = acum[:, e:e + 1] - acum_t[e:e + 1, :]
        decay = jnp.where(causal, jnp.exp(jnp.minimum(seg, 0.0)), 0.0)
        return jnp.dot((cb * decay).astype(BF16), x_pair, preferred_element_type=F32)

    first_half = lax.broadcasted_iota(jnp.int32, (ln, 2 * p), 1) < p
    parts = []
    for pair in range(gh // 2):
        x_pair = xdt_b[:, pair * 2 * p:(pair + 1) * 2 * p]
        parts.append(jnp.where(first_half, head_diag(2 * pair, x_pair), head_diag(2 * pair + 1, x_pair)))
    y_diag = jnp.concatenate(parts, axis=1)

    state = state_ref[...]
    y_off = jnp.dot(cm_b, state.astype(BF16), preferred_element_type=F32) * jnp.exp(acum_x)
    to_end = jnp.exp(last_x - acum_x)
    upd = jnp.dot(bm.T.astype(BF16), (xdt * to_end).astype(BF16), preferred_element_type=F32)
    state_ref[...] = state * jnp.exp(last_x) + upd

    y = y_diag + y_off + dskip_ref[...] * xs
    zf = z_ref[...].astype(F32)
    y = y * (zf * _sigmoid(zf))
    y = y * lax.rsqrt(jnp.mean(y * y, axis=-1, keepdims=True) + RMS_EPS)
    o_ref[...] = (y * nw_ref[...]).astype(o_ref.dtype)


def _ssd_mixer(h, dt_raw, conv_w, conv_b, dt_bias, a_log, d_skip, norm_w, batch, seq,
               z_block, x_block, b_block, c_block):
    ln = min(SSD_CHUNK, seq)
    nc = seq // ln
    hb = ln // HALO
    gd, gh = SSD_GROUP_DIM, SSD_GROUP_HEADS
    ngb = D_SSD // SSD_STATE

    def cur(base):
        return lambda b, g, c: (b * nc + c, base + g)

    def halo(base):
        return lambda b, g, c: (jnp.maximum((b * nc + c) * hb - 1, 0), base + g)

    grp = lambda b, g, c: (g, 0, 0)
    in_specs = [
        pl.BlockSpec((ln, gd), cur(z_block)),
        pl.BlockSpec((ln, gd), cur(x_block)), pl.BlockSpec((HALO, gd), halo(x_block)),
        pl.BlockSpec((ln, SSD_STATE), cur(b_block)), pl.BlockSpec((HALO, SSD_STATE), halo(b_block)),
        pl.BlockSpec((ln, SSD_STATE), cur(c_block)), pl.BlockSpec((HALO, SSD_STATE), halo(c_block)),
        pl.BlockSpec((None, ln, V7X_LANES), lambda b, g, c: (g, b * nc + c, 0)),
        pl.BlockSpec((SSD_CONV, gd), lambda b, g, c: (0, g)), pl.BlockSpec((1, gd), lambda b, g, c: (0, g)),
        pl.BlockSpec((SSD_CONV, SSD_STATE), lambda b, g, c: (0, ngb + g)),
        pl.BlockSpec((1, SSD_STATE), lambda b, g, c: (0, ngb + g)),
        pl.BlockSpec((SSD_CONV, SSD_STATE), lambda b, g, c: (0, ngb + SSD_GROUPS + g)),
        pl.BlockSpec((1, SSD_STATE), lambda b, g, c: (0, ngb + SSD_GROUPS + g)),
        pl.BlockSpec((None, 1, V7X_LANES), grp), pl.BlockSpec((None, 1, V7X_LANES), grp),
        pl.BlockSpec((1, gd), lambda b, g, c: (0, g)), pl.BlockSpec((1, gd), lambda b, g, c: (0, g)),
    ]
    conv_b2 = conv_b.reshape(1, -1)
    d_skip_x = jnp.repeat(d_skip.astype(F32), SSD_HEADDIM).reshape(1, D_SSD)

    def per_head(v):
        return jnp.pad(v.astype(F32).reshape(SSD_GROUPS, 1, gh), ((0, 0), (0, 0), (0, V7X_LANES - gh)))
    return pl.pallas_call(
        functools.partial(_ssd_body, ln=ln),
        grid=(batch, SSD_GROUPS, nc),
        in_specs=in_specs,
        out_specs=pl.BlockSpec((ln, gd), lambda b, g, c: (b * nc + c, g)),
        out_shape=jax.ShapeDtypeStruct((batch * seq, D_SSD), BF16),
        scratch_shapes=[pltpu.VMEM((HALO + ln, gd), F32), pltpu.VMEM((SSD_STATE, gd), F32)],
        compiler_params=_params(("parallel", "parallel", "arbitrary")),
        name="ssd_mixer",
    )(h, h, h, h, h, h, h, dt_raw,
      conv_w, conv_b2, conv_w, conv_b2, conv_w, conv_b2,
      per_head(dt_bias), per_head(a_log), d_skip_x, norm_w.reshape(1, D_SSD))


def _gconv_body(bg_ref, cg_ref, cgh_ref, xi_ref, xih_ref, w_ref, o_ref, ext_ref, *, ts):
    i = pl.program_id(1)
    keep = jnp.where(i > 0, 1.0, 0.0).astype(F32)
    ext_ref[0:HALO, :] = cgh_ref[...].astype(F32) * xih_ref[...].astype(F32) * keep
    ext_ref[HALO:HALO + ts, :] = cg_ref[...].astype(F32) * xi_ref[...].astype(F32)
    acc = jnp.zeros((ts, D_CONV), F32)
    for tap in range(CONV_WIDTH):
        off = HALO - (CONV_WIDTH - 1) + tap
        acc = acc + w_ref[tap:tap + 1, :] * ext_ref[off:off + ts, :]
    o_ref[...] = (bg_ref[...].astype(F32) * acc).astype(o_ref.dtype)


def _gated_conv(h, conv_w, batch, seq):
    ts = min(CONV_TILE, seq)
    nt = seq // ts
    hb = ts // HALO

    def cur(blk):
        return lambda b, i: (b * nt + i, blk)

    def halo(blk):
        return lambda b, i: (jnp.maximum((b * nt + i) * hb - 1, 0), blk)

    return pl.pallas_call(
        functools.partial(_gconv_body, ts=ts),
        grid=(batch, nt),
        in_specs=[pl.BlockSpec((ts, D_CONV), cur(0)),
                  pl.BlockSpec((ts, D_CONV), cur(1)), pl.BlockSpec((HALO, D_CONV), halo(1)),
                  pl.BlockSpec((ts, D_CONV), cur(2)), pl.BlockSpec((HALO, D_CONV), halo(2)),
                  pl.BlockSpec((CONV_WIDTH, D_CONV), lambda b, i: (0, 0))],
        out_specs=pl.BlockSpec((ts, D_CONV), lambda b, i: (b * nt + i, 0)),
        out_shape=jax.ShapeDtypeStruct((batch * seq, D_CONV), BF16),
        scratch_shapes=[pltpu.VMEM((HALO + ts, D_CONV), F32)],
        compiler_params=_params(("parallel", "parallel")),
        name="gated_conv",
    )(h, h, h, h, h, conv_w)


def _sb_block(q, k_ref, v_ref, start, tk, carry, acc, upper, mask):
    kt = k_ref[pl.ds(start, tk), :]
    vt = v_ref[pl.ds(start, tk), :]
    z = lax.dot_general(q, kt, (((1,), (1,)), ((), ())), preferred_element_type=F32) * (SB_HEADDIM ** -0.5)
    sp = _softplus(z)
    log_stay = -sp if mask is None else jnp.where(mask, -sp, 0.0)
    hi = log_stay.astype(BF16)
    lo = (log_stay - hi.astype(F32)).astype(BF16)
    after = (jnp.dot(hi, upper, preferred_element_type=F32)
             + jnp.dot(lo, upper, preferred_element_type=F32)) + carry
    w = jnp.exp((z - sp) + after)
    if mask is not None:
        w = jnp.where(mask, w, 0.0)
    acc = acc + jnp.dot(w.astype(BF16), vt, preferred_element_type=F32)
    carry = carry + jnp.sum(log_stay, axis=1, keepdims=True)
    return carry, acc


def _sb_body(q_ref, k_ref, v_ref, o_ref, *, tq):
    qi = pl.program_id(2)
    q = q_ref[...]
    row = lax.broadcasted_iota(jnp.int32, (tq, tq), 0)
    col = lax.broadcasted_iota(jnp.int32, (tq, tq), 1)
    upper = jnp.where(row > col, 1.0, 0.0).astype(BF16)
    diag_mask = col < row

    def tile(j, carry, acc, mask):
        return _sb_block(q, k_ref, v_ref, pl.multiple_of(j * tq, tq), tq, carry, acc, upper, mask)

    def tile_pair(j, carry, acc):
        carry, acc = tile(j, carry, acc, None)
        has_second = jnp.broadcast_to(j >= 1, (tq, tq))
        return tile(jnp.maximum(j - 1, 0), carry, acc, has_second)

    carry = jnp.zeros((tq, 1), F32)
    acc = jnp.zeros((tq, SB_HEADDIM), F32)
    carry, acc = tile(qi, carry, acc, diag_mask)
    has_prev = jnp.broadcast_to(qi >= 1, (tq, tq))
    carry, acc = tile(jnp.maximum(qi - 1, 0), carry, acc, has_prev)

    def more(state):
        j, carry, _ = state
        return jnp.logical_and(j >= 0, jnp.max(carry) > SB_UNDERFLOW_LOG)

    def step(state):
        j, carry, acc = state
        carry, acc = tile_pair(j, carry, acc)
        return j - 2, carry, acc

    _, _, acc = lax.while_loop(more, step, (qi - 2, carry, acc))
    o_ref[...] = acc.astype(o_ref.dtype)


def _sb_attention(h, q_block, k_block, v_block, batch, seq):
    tq = min(SB_TILE, seq)
    nq = seq // tq
    return pl.pallas_call(
        functools.partial(_sb_body, tq=tq),
        grid=(batch, SB_HEADS, nq),
        in_specs=[pl.BlockSpec((tq, SB_HEADDIM), lambda b, hd, i: (b * nq + i, q_block + hd)),
                  pl.BlockSpec((seq, SB_HEADDIM), lambda b, hd, i: (b, k_block + hd)),
                  pl.BlockSpec((seq, SB_HEADDIM), lambda b, hd, i: (b, v_block + hd))],
        out_specs=pl.BlockSpec((tq, SB_HEADDIM), lambda b, hd, i: (b * nq + i, hd)),
        out_shape=jax.ShapeDtypeStruct((batch * seq, D_SB), BF16),
        compiler_params=_params(("parallel", "parallel", "arbitrary")),
        name="stick_breaking_attention",
    )(h, h, h)


def _router_body(x_ref, wt_ref, b_ref, idx_ref, wgt_ref):
    logits = lax.dot_general(wt_ref[...], x_ref[...], (((1,), (1,)), ((), ())),
                             precision=lax.Precision.HIGHEST, preferred_element_type=F32) + b_ref[...]
    iota = lax.broadcasted_iota(jnp.int32, logits.shape, 0).astype(F32)
    vals, idxs = [], []
    for _ in range(TOP_K):
        mx = jnp.max(logits, axis=0, keepdims=True)
        ix = jnp.min(jnp.where(logits == mx, iota, float(N_EXPERTS)), axis=0, keepdims=True)
        vals.append(mx)
        idxs.append(ix)
        logits = jnp.where(iota == ix, -jnp.inf, logits)
    exps = [jnp.exp(v - vals[0]) for v in vals]
    denom = exps[0] + exps[1] + exps[2] + exps[3]
    for k in range(TOP_K):
        idx_ref[k:k + 1, :] = idxs[k].astype(jnp.int32)
        wgt_ref[k:k + 1, :] = exps[k] / denom


def _router(x, router_w, router_b):
    t, d = x.shape
    tm = min(ROUTER_TILE, t)
    out = pl.BlockSpec((TOP_K, tm), lambda i: (0, i))
    return pl.pallas_call(
        _router_body,
        grid=(t // tm,),
        in_specs=[pl.BlockSpec((tm, d), lambda i: (i, 0)),
                  pl.BlockSpec((N_EXPERTS, d), lambda i: (0, 0)),
                  pl.BlockSpec((N_EXPERTS, 1), lambda i: (0, 0))],
        out_specs=[out, out],
        out_shape=[jax.ShapeDtypeStruct((TOP_K, t), jnp.int32), jax.ShapeDtypeStruct((TOP_K, t), F32)],
        compiler_params=_params(("parallel",)),
        name="moe_router",
    )(x, router_w.T, router_b.reshape(N_EXPERTS, 1))


def _routing_tables(idx, tm):
    t = idx.shape[1]
    pairs = TOP_K * t
    rows = pairs + N_EXPERTS * tm
    ntiles = rows // tm
    e_flat = idx.reshape(pairs)
    onehot = (e_flat[:, None] == jnp.arange(N_EXPERTS, dtype=jnp.int32)[None, :]).astype(jnp.int32)
    csum = jnp.cumsum(onehot, axis=0)
    rank = jnp.sum((csum - onehot) * onehot, axis=1)
    counts = csum[-1]
    padded = ((counts + tm - 1) // tm) * tm
    gend = jnp.cumsum(padded)
    gstart = gend - padded
    pos = (gstart[e_flat] + rank).astype(jnp.int32)
    token = (jnp.arange(pairs, dtype=jnp.int32) % t)
    tok_of_row = jnp.zeros((rows,), jnp.int32).at[pos].set(token)
    tile_start = jnp.arange(ntiles, dtype=jnp.int32) * tm
    tile_expert = jnp.minimum(jnp.searchsorted(gend, tile_start, side="right"), N_EXPERTS - 1).astype(jnp.int32)
    tile_valid = (tile_start < gend[-1]).astype(jnp.int32)
    pos_tok = pos.reshape(TOP_K, t).T.reshape(pairs)
    return tok_of_row, pos_tok, tile_expert, tile_valid


def _experts_body(te_ref, tv_ref, tok_ref, x_hbm, w1_ref, b1_ref, w2_ref, b2_ref, o_ref,
                  xbuf, acc_ref, sem, *, tm):
    i = pl.program_id(0)
    nt = pl.num_programs(0)
    half = x_hbm.shape[1]
    kc = half // MOE_ISSUE_CHUNKS
    burst = tm // MOE_ISSUE_CHUNKS

    def row_copy(tok, slot, r):
        return pltpu.make_async_copy(x_hbm.at[pl.ds(tok, 1), :], xbuf.at[slot, pl.ds(r, 1), :], sem.at[slot])

    def issue(tile, slot, first, count):
        base = tile * tm

        def body(r, carry):
            row_copy(tok_ref[base + r], slot, r).start()
            return carry

        lax.fori_loop(first, first + count, body, 0, unroll=8)

    @pl.when(i == 0)
    def _():
        issue(0, 0, 0, tm)

    slot = i % 2
    nxt = (i + 1) % 2
    has_next = i + 1 < nt
    pltpu.make_async_copy(x_hbm.at[pl.ds(0, tm), :], xbuf.at[slot], sem.at[slot]).wait()

    for c in range(MOE_ISSUE_CHUNKS):
        @pl.when(has_next)
        def _():
            issue(i + 1, nxt, c * burst, burst)

        low, high = _unpack_halves(xbuf[slot, :, c * kc:(c + 1) * kc])
        part = (jnp.dot(low.astype(BF16), w1_ref[c * kc:(c + 1) * kc, :], preferred_element_type=F32)
                + jnp.dot(high.astype(BF16), w1_ref[half + c * kc:half + (c + 1) * kc, :],
                          preferred_element_type=F32))
        if c == 0:
            acc_ref[...] = part
        else:
            acc_ref[...] += part

    @pl.when(tv_ref[i] == 1)
    def _():
        hh = acc_ref[...] + b1_ref[...]
        glu = jnp.minimum(hh[:, :D_EXPERT], SWIGLU_LIMIT)
        lin = jnp.clip(hh[:, D_EXPERT:], -SWIGLU_LIMIT, SWIGLU_LIMIT)
        act = glu * _sigmoid(SWIGLU_ALPHA * glu) * (lin + 1.0)
        out = jnp.dot(act.astype(BF16), w2_ref[...], preferred_element_type=F32) + b2_ref[...]
        o_ref[...] = _pack_halves(out)

    @pl.when(tv_ref[i] == 0)
    def _():
        o_ref[...] = jnp.zeros_like(o_ref)


def _experts(x_packed, w1, b1, w2, b2, tok_of_row, tile_expert, tile_valid, tm):
    t, half = x_packed.shape
    d = 2 * half
    rows = tok_of_row.shape[0]
    ntiles = rows // tm
    grid_spec = pltpu.PrefetchScalarGridSpec(
        num_scalar_prefetch=3,
        grid=(ntiles,),
        in_specs=[pl.BlockSpec(memory_space=pl.ANY),
                  pl.BlockSpec((None, d, 2 * D_EXPERT), lambda i, te, tv, tok: (te[i], 0, 0)),
                  pl.BlockSpec((None, 1, 2 * D_EXPERT), lambda i, te, tv, tok: (te[i], 0, 0)),
                  pl.BlockSpec((None, D_EXPERT, d), lambda i, te, tv, tok: (te[i], 0, 0)),
                  pl.BlockSpec((None, 1, d), lambda i, te, tv, tok: (te[i], 0, 0))],
        out_specs=pl.BlockSpec((tm, half), lambda i, te, tv, tok: (i, 0)),
        scratch_shapes=[pltpu.VMEM((2, tm, half), jnp.uint32), pltpu.VMEM((tm, 2 * D_EXPERT), F32),
                        pltpu.SemaphoreType.DMA((2,))],
    )
    return pl.pallas_call(
        functools.partial(_experts_body, tm=tm),
        grid_spec=grid_spec,
        out_shape=jax.ShapeDtypeStruct((rows, half), jnp.uint32),
        compiler_params=_params(("arbitrary",)),
        name="moe_experts",
    )(tile_expert, tile_valid, tok_of_row, x_packed, w1, b1.reshape(N_EXPERTS, 1, -1), w2,
      b2.reshape(N_EXPERTS, 1, -1))


def _combine_body(pos_ref, y_hbm, x_ref, wgt_ref, g_ref, b_ref, of_ref, *rest, tm, emit_bf16):
    ob_ref = rest[0] if emit_bf16 else None
    ybuf, sem = rest[-2:]
    i = pl.program_id(0)
    nt = pl.num_programs(0)

    def row_copy(src, slot, k, r):
        return pltpu.make_async_copy(y_hbm.at[pl.ds(src, 1), :], ybuf.at[slot, k, pl.ds(r, 1), :], sem.at[slot])

    def issue(tile, slot):
        base = tile * tm * TOP_K

        def body(r, carry):
            for k in range(TOP_K):
                row_copy(pos_ref[base + r * TOP_K + k], slot, k, r).start()
            return carry

        lax.fori_loop(0, tm, body, 0, unroll=4)

    @pl.when(i == 0)
    def _():
        issue(0, 0)

    @pl.when(i + 1 < nt)
    def _():
        issue(i + 1, (i + 1) % 2)

    slot = i % 2
    for k in range(TOP_K):
        pltpu.make_async_copy(y_hbm.at[pl.ds(0, tm), :], ybuf.at[slot, k], sem.at[slot]).wait()
    wgt = wgt_ref[...]
    moe_low, moe_high = None, None
    for k in range(TOP_K):
        low, high = _unpack_halves(ybuf[slot, k])
        gk = wgt[:, k:k + 1]
        moe_low = gk * low if k == 0 else moe_low + gk * low
        moe_high = gk * high if k == 0 else moe_high + gk * high
    moe = jnp.concatenate([moe_low, moe_high], axis=1)
    y = _layer_norm_rows(DEEPNORM_ALPHA * x_ref[...] + moe, g_ref[...], b_ref[...])
    of_ref[...] = y
    if emit_bf16:
        ob_ref[...] = y.astype(BF16)


def _combine_layer_norm(x, y_rows, pos_tok, wgt_tok, g, b, tm, emit_bf16):
    t, d = x.shape
    half = d // 2
    row = lambda i, pos: (i, 0)
    vec = lambda i, pos: (0, 0)
    out_specs = [pl.BlockSpec((tm, d), row)]
    out_shape = [jax.ShapeDtypeStruct((t, d), F32)]
    if emit_bf16:
        out_specs.append(pl.BlockSpec((tm, d), row))
        out_shape.append(jax.ShapeDtypeStruct((t, d), BF16))
    grid_spec = pltpu.PrefetchScalarGridSpec(
        num_scalar_prefetch=1,
        grid=(t // tm,),
        in_specs=[pl.BlockSpec(memory_space=pl.ANY),
                  pl.BlockSpec((tm, d), row), pl.BlockSpec((tm, TOP_K), row),
                  pl.BlockSpec((1, d), vec), pl.BlockSpec((1, d), vec)],
        out_specs=out_specs,
        scratch_shapes=[pltpu.VMEM((2, TOP_K, tm, half), jnp.uint32), pltpu.SemaphoreType.DMA((2,))],
    )
    return pl.pallas_call(
        functools.partial(_combine_body, tm=tm, emit_bf16=emit_bf16),
        grid_spec=grid_spec,
        out_shape=out_shape,
        compiler_params=_params(("arbitrary",)),
        name="moe_combine_layer_norm",
    )(pos_tok, y_rows, x, wgt_tok, g.reshape(1, d), b.reshape(1, d))


def _moe_block(x_f32, x_packed, router_w, router_b, w1, b1, w2, b2, g, b, emit_bf16):
    t = x_f32.shape[0]
    tm = min(MOE_TILE, t)
    idx, wgt = _router(x_f32, router_w, router_b)
    tok_of_row, pos_tok, tile_expert, tile_valid = _routing_tables(idx, tm)
    y_rows = _experts(x_packed, w1.astype(BF16), b1, w2.astype(BF16), b2, tok_of_row, tile_expert, tile_valid, tm)
    return _combine_layer_norm(x_f32, y_rows, pos_tok, wgt.T, g, b, min(COMBINE_TILE, t), emit_bf16)


def _layer0_mixer(x_f32, x_b16, batch, seq, w_in, conv_w, conv_b, dt_bias, a_log, d_skip, norm_w,
                  pool_w, pool_scale, w_out, ln_g, ln_b):
    o_u, o_z, o_xbc, o_dt = 0, D_POOL, D_POOL + D_SSD, D_POOL + D_SSD + D_SSD + 2 * SSD_BC
    w_main = jnp.concatenate([w_in[:, o_z:o_xbc], w_in[:, o_xbc:o_dt], w_in[:, o_u:o_z]], axis=1).astype(BF16)
    w_dt = jnp.pad(w_in[:, o_dt:], ((0, 0), (0, V7X_LANES - SSD_HEADS))).astype(BF16)
    h = _matmul(x_b16, w_main, BF16, 1024, 1024, D_MODEL)
    dt_raw = _matmul(x_b16, w_dt, F32, 1024, V7X_LANES, D_MODEL)[:, :SSD_HEADS]
    t = batch * seq
    dt_raw = dt_raw.reshape(t, SSD_GROUPS, SSD_GROUP_HEADS).transpose(1, 0, 2)
    dt_raw = jnp.pad(dt_raw, ((0, 0), (0, 0), (0, V7X_LANES - SSD_GROUP_HEADS)))
    y_ssd = _ssd_mixer(h, dt_raw, conv_w, conv_b, dt_bias, a_log, d_skip, norm_w, batch, seq,
                       z_block=0, x_block=D_SSD // SSD_GROUP_DIM,
                       b_block=2 * D_SSD // SSD_STATE, c_block=(2 * D_SSD + SSD_BC) // SSD_STATE)
    y_pool = _pool_mixer(h, (2 * D_SSD + 2 * SSD_BC) // D_POOL, pool_w.astype(BF16), pool_scale, batch, seq)
    y = jnp.concatenate([y_pool, y_ssd], axis=1)
    m = _matmul(y, w_out.astype(BF16), F32, 1024, 1024, D_MODEL)
    return _residual_layer_norm(x_f32, m, ln_g, ln_b)


def _layer1_mixer(x_f32, x_b16, batch, seq, w_in, conv_w, w_out, ln_g, ln_b):
    h = _matmul(x_b16, w_in.astype(BF16), BF16, 1024, 1024, D_MODEL)
    y_conv = _gated_conv(h, conv_w, batch, seq)
    qb = 3 * D_CONV // SB_HEADDIM
    y_sb = _sb_attention(h, qb, qb + SB_HEADS, qb + 2 * SB_HEADS, batch, seq)
    y = jnp.concatenate([y_conv, y_sb], axis=1)
    m = _matmul(y, w_out.astype(BF16), F32, 1024, 1024, D_MODEL)
    return _residual_layer_norm(x_f32, m, ln_g, ln_b)


def kernel(x, l0_w_in, l0_conv_w, l0_conv_b, l0_dt_bias, l0_a_log, l0_d_skip, l0_ssm_norm_w, l0_pool_w, l0_pool_scale, l0_w_out, l0_ln_mix_g, l0_ln_mix_b, l0_router_w, l0_router_b, l0_w1, l0_b1, l0_w2, l0_b2, l0_ln_ffn_g, l0_ln_ffn_b, l1_w_in, l1_conv_w, l1_w_out, l1_ln_mix_g, l1_ln_mix_b, l1_router_w, l1_router_b, l1_w1, l1_b1, l1_w2, l1_b2, l1_ln_ffn_g, l1_ln_ffn_b):
    batch, seq, d = x.shape
    xf = x.reshape(batch * seq, d)
    xb = xf.astype(BF16)
    xf, xp = _layer0_mixer(xf, xb, batch, seq, l0_w_in, l0_conv_w, l0_conv_b, l0_dt_bias, l0_a_log, l0_d_skip,
                           l0_ssm_norm_w, l0_pool_w, l0_pool_scale, l0_w_out, l0_ln_mix_g, l0_ln_mix_b)
    xf, xb = _moe_block(xf, xp, l0_router_w, l0_router_b, l0_w1, l0_b1, l0_w2, l0_b2, l0_ln_ffn_g, l0_ln_ffn_b,
                        emit_bf16=True)
    xf, xp = _layer1_mixer(xf, xb, batch, seq, l1_w_in, l1_conv_w, l1_w_out, l1_ln_mix_g, l1_ln_mix_b)
    (xf,) = _moe_block(xf, xp, l1_router_w, l1_router_b, l1_w1, l1_b1, l1_w2, l1_b2, l1_ln_ffn_g, l1_ln_ffn_b,
                       emit_bf16=False)
    return xf.reshape(batch, seq, d)
```

```python
import functools

import jax
import jax.numpy as jnp
from jax import lax
from jax.experimental import pallas as pl
from jax.experimental.pallas import tpu as pltpu

F32 = jnp.float32
BF16 = jnp.bfloat16

D_MODEL = 4096
DEPTH = 2
DEEPNORM_ALPHA = (2.0 * DEPTH) ** 0.25
LN_EPS = 1e-5
RMS_EPS = 1e-5

POOL_WINDOWS = (2, 4, 8, 16)
POOL_GROUPS = 4
D_POOL = D_MODEL // 2
POOL_GROUP_DIM = D_POOL // POOL_GROUPS

SSD_HEADDIM = 64
D_SSD = (3 * D_MODEL) // 2
SSD_HEADS = D_SSD // SSD_HEADDIM
SSD_GROUPS = 8
SSD_STATE = 128
SSD_CONV = 4
SSD_GROUP_HEADS = SSD_HEADS // SSD_GROUPS
SSD_GROUP_DIM = D_SSD // SSD_GROUPS
SSD_BC = SSD_GROUPS * SSD_STATE

D_CONV = D_MODEL // 2
CONV_WIDTH = 3
SB_HEADS = 16
SB_HEADDIM = 128
D_SB = SB_HEADS * SB_HEADDIM

N_EXPERTS = 32
TOP_K = 4
D_EXPERT = 512
SWIGLU_LIMIT = 7.0
SWIGLU_ALPHA = 1.702

V7X_LANES = 128
V7X_BF16_SUBLANES = 16
V7X_VMEM_LIMIT = 56 * 1024 * 1024
HALO = V7X_BF16_SUBLANES

SSD_CHUNK = 256
POOL_TILE = 512
CONV_TILE = 512
LN_TILE = 256
ROUTER_TILE = 512
SB_TILE = 256
SB_UNDERFLOW_LOG = -110.0
MOE_TILE = 512
MOE_ISSUE_UNROLL = 8
COMBINE_TILE = 128


def _params(semantics):
    return pltpu.CompilerParams(dimension_semantics=semantics, vmem_limit_bytes=V7X_VMEM_LIMIT)


def _softplus(x):
    return jnp.maximum(x, 0.0) + jnp.log(1.0 + jnp.exp(-jnp.abs(x)))


def _sigmoid(x):
    return 1.0 / (1.0 + jnp.exp(-x))


def _bf16_terms(v, terms):
    out = []
    for _ in range(terms - 1):
        piece = v.astype(BF16)
        out.append(piece)
        v = v - piece.astype(F32)
    out.append(v.astype(BF16))
    return out


def _dot_with_01(v, mat01, terms, v_on_left=True):
    parts = _bf16_terms(v, terms)
    if v_on_left:
        dots = [jnp.dot(p, mat01, preferred_element_type=F32) for p in parts]
    else:
        dots = [jnp.dot(mat01, p, preferred_element_type=F32) for p in parts]
    total = dots[0]
    for d in dots[1:]:
        total = total + d
    return total


def _mm_body(a_ref, w_ref, o_ref, wb_ref):
    @pl.when(pl.program_id(1) == 0)
    def _():
        wb_ref[...] = w_ref[...].astype(BF16)

    o_ref[...] = jnp.dot(a_ref[...], wb_ref[...], preferred_element_type=F32).astype(o_ref.dtype)


def _matmul(a, w, out_dtype, tm, tn, col0=0, ncols=None):
    m, kdim = a.shape
    n = w.shape[1] - col0 if ncols is None else ncols
    tm, tn = min(tm, m), min(tn, n)
    assert m % tm == 0 and n % tn == 0 and col0 % tn == 0 and w.shape[0] == kdim
    cb0 = col0 // tn
    return pl.pallas_call(
        _mm_body,
        grid=(n // tn, m // tm),
        in_specs=[pl.BlockSpec((tm, kdim), lambda j, i: (i, 0)),
                  pl.BlockSpec((kdim, tn), lambda j, i: (0, cb0 + j))],
        out_specs=pl.BlockSpec((tm, tn), lambda j, i: (i, j)),
        out_shape=jax.ShapeDtypeStruct((m, n), out_dtype),
        scratch_shapes=[pltpu.VMEM((kdim, tn), BF16)],
        compiler_params=_params(("parallel", "arbitrary")),
        name="matmul",
    )(a, w)


def _mm_cat_body(a1_ref, a2_ref, w_ref, o_ref, acc_ref, *, n1, nk):
    k = pl.program_id(2)

    def accumulate(a_ref):
        part = jnp.dot(a_ref[...], w_ref[...].astype(BF16), preferred_element_type=F32)

        @pl.when(k == 0)
        def _():
            acc_ref[...] = part

        @pl.when(k > 0)
        def _():
            acc_ref[...] += part

    @pl.when(k < n1)
    def _():
        accumulate(a1_ref)

    @pl.when(k >= n1)
    def _():
        accumulate(a2_ref)

    @pl.when(k == nk - 1)
    def _():
        o_ref[...] = acc_ref[...].astype(o_ref.dtype)


def _matmul_cat(a1, a2, w, out_dtype, tm, tn, tk):
    m, k1 = a1.shape
    k2 = a2.shape[1]
    n = w.shape[1]
    assert k1 % tk == 0 and k2 % tk == 0 and m % tm == 0 and n % tn == 0 and w.shape[0] == k1 + k2
    n1, nk = k1 // tk, (k1 + k2) // tk
    return pl.pallas_call(
        functools.partial(_mm_cat_body, n1=n1, nk=nk),
        grid=(n // tn, m // tm, nk),
        in_specs=[pl.BlockSpec((tm, tk), lambda j, i, k: (i, jnp.minimum(k, n1 - 1))),
                  pl.BlockSpec((tm, tk), lambda j, i, k: (i, jnp.maximum(k - n1, 0))),
                  pl.BlockSpec((tk, tn), lambda j, i, k: (k, j))],
        out_specs=pl.BlockSpec((tm, tn), lambda j, i, k: (i, j)),
        out_shape=jax.ShapeDtypeStruct((m, n), out_dtype),
        scratch_shapes=[pltpu.VMEM((tm, tn), F32)],
        compiler_params=_params(("parallel", "parallel", "arbitrary")),
        name="matmul_cat",
    )(a1, a2, w)


def _layer_norm_rows(v, g, b):
    mu = jnp.mean(v, axis=-1, keepdims=True)
    vc = v - mu
    var = jnp.mean(vc * vc, axis=-1, keepdims=True)
    return vc * lax.rsqrt(var + LN_EPS) * g + b


def _bf16_bits(v):
    return lax.bitcast_convert_type(v.astype(BF16).astype(F32), jnp.uint32)


def _pack_halves(v):
    n = v.shape[1] // 2
    return _bf16_bits(v[:, n:]) | (_bf16_bits(v[:, :n]) >> 16)


def _unpack_halves(words):
    low = lax.bitcast_convert_type(words << 16, F32)
    high = lax.bitcast_convert_type(words & jnp.uint32(0xFFFF0000), F32)
    return low, high


def _ln_body(x_ref, m_ref, g_ref, b_ref, of_ref, op_ref):
    v = DEEPNORM_ALPHA * x_ref[...] + m_ref[...].astype(F32)
    y = _layer_norm_rows(v, g_ref[...], b_ref[...])
    of_ref[...] = y
    op_ref[...] = _pack_halves(y)


def _residual_layer_norm(x, m, g, b):
    t, d = x.shape
    tm = min(LN_TILE, t)
    row = pl.BlockSpec((tm, d), lambda i: (i, 0))
    half = pl.BlockSpec((tm, d // 2), lambda i: (i, 0))
    vec = pl.BlockSpec((1, d), lambda i: (0, 0))
    return pl.pallas_call(
        _ln_body,
        grid=(t // tm,),
        in_specs=[row, row, vec, vec],
        out_specs=[row, half],
        out_shape=[jax.ShapeDtypeStruct((t, d), F32), jax.ShapeDtypeStruct((t, d // 2), jnp.uint32)],
        compiler_params=_params(("parallel",)),
        name="residual_layer_norm",
    )(x, m, g.reshape(1, d), b.reshape(1, d))


def _pool_body(u_ref, halo_ref, w_ref, scale_ref, o_ref, ext_ref, *, ts):
    i = pl.program_id(1)
    keep = jnp.where(i > 0, 1.0, 0.0).astype(F32)
    pos = (i * ts + lax.broadcasted_iota(jnp.int32, (ts, 1), 0) + 1).astype(F32)
    for g, win in enumerate(POOL_WINDOWS):
        cols = slice(g * POOL_GROUP_DIM, (g + 1) * POOL_GROUP_DIM)
        cur = u_ref[:, cols].astype(F32)
        ext_ref[0:HALO, :] = halo_ref[:, cols].astype(F32) * keep
        ext_ref[HALO:HALO + ts, :] = cur
        acc = cur
        for back in range(1, win):
            acc = acc + ext_ref[HALO - back:HALO - back + ts, :]
        mean = acc / jnp.minimum(pos, float(win))
        mixed = jnp.dot((mean - cur).astype(BF16), w_ref[g], preferred_element_type=F32)
        o_ref[:, cols] = (mixed * scale_ref[:, cols]).astype(o_ref.dtype)


def _pool_mixer(h, col_block, pool_w, pool_scale, batch, seq):
    ts = min(POOL_TILE, seq)
    nt = seq // ts
    hb = ts // HALO

    def cur_map(b, i):
        return (b * nt + i, col_block)

    def halo_map(b, i):
        return (jnp.maximum((b * nt + i) * hb - 1, 0), col_block)

    return pl.pallas_call(
        functools.partial(_pool_body, ts=ts),
        grid=(batch, nt),
        in_specs=[pl.BlockSpec((ts, D_POOL), cur_map),
                  pl.BlockSpec((HALO, D_POOL), halo_map),
                  pl.BlockSpec((POOL_GROUPS, POOL_GROUP_DIM, POOL_GROUP_DIM), lambda b, i: (0, 0, 0)),
                  pl.BlockSpec((1, D_POOL), lambda b, i: (0, 0))],
        out_specs=pl.BlockSpec((ts, D_POOL), lambda b, i: (b * nt + i, 0)),
        out_shape=jax.ShapeDtypeStruct((batch * seq, D_POOL), BF16),
        scratch_shapes=[pltpu.VMEM((HALO + ts, POOL_GROUP_DIM), F32)],
        compiler_params=_params(("parallel", "parallel")),
        name="pool_mixer",
    )(h, h, pool_w, pool_scale.reshape(1, D_POOL))


def _conv_silu(cur_ref, halo_ref, w_ref, b_ref, ext_ref, keep, width, ln):
    ext_ref[0:HALO, 0:width] = halo_ref[...].astype(F32) * keep
    ext_ref[HALO:HALO + ln, 0:width] = cur_ref[...].astype(F32)
    acc = b_ref[...]
    for tap in range(SSD_CONV):
        off = HALO - (SSD_CONV - 1) + tap
        acc = acc + w_ref[tap:tap + 1, :] * ext_ref[off:off + ln, 0:width]
    return acc * _sigmoid(acc)


def _ssd_body(z_ref, x_ref, xh_ref, b_ref, bh_ref, c_ref, ch_ref, dt_ref,
              cwx_ref, cbx_ref, cwb_ref, cbb_ref, cwc_ref, cbc_ref,
              dtb_ref, alog_ref, dskip_ref, nw_ref, o_ref, ext_ref, state_ref, *, ln):
    c = pl.program_id(2)
    keep = jnp.where(c > 0, 1.0, 0.0).astype(F32)

    @pl.when(c == 0)
    def _():
        state_ref[...] = jnp.zeros_like(state_ref)

    gh, p, gd = SSD_GROUP_HEADS, SSD_HEADDIM, SSD_GROUP_DIM

    xs = _conv_silu(x_ref, xh_ref, cwx_ref, cbx_ref, ext_ref, keep, gd, ln)
    bm = _conv_silu(b_ref, bh_ref, cwb_ref, cbb_ref, ext_ref, keep, SSD_STATE, ln)
    cm = _conv_silu(c_ref, ch_ref, cwc_ref, cbc_ref, ext_ref, keep, SSD_STATE, ln)

    dt = _softplus(dt_ref[...] + dtb_ref[...])
    da = dt * (-jnp.exp(alog_ref[...]))

    row = lax.broadcasted_iota(jnp.int32, (ln, ln), 0)
    col = lax.broadcasted_iota(jnp.int32, (ln, ln), 1)
    causal = row >= col
    tri = jnp.where(causal, 1.0, 0.0).astype(BF16)
    acum = _dot_with_01(da, tri, 3, v_on_left=False)
    acum_t = acum.T

    eh = lax.broadcasted_iota(jnp.int32, (V7X_LANES, gd), 0)
    ec = lax.broadcasted_iota(jnp.int32, (V7X_LANES, gd), 1)
    expand = jnp.where((ec >= eh * p) & (ec < (eh + 1) * p), 1.0, 0.0).astype(BF16)
    dt_x = _dot_with_01(dt, expand, 2)
    acum_x = _dot_with_01(acum, expand, 3)
    last_x = acum_x[ln - 1:ln, :]

    xdt = xs * dt_x
    xdt_b = xdt.astype(BF16)
    bm_b = bm.astype(BF16)
    cm_b = cm.astype(BF16)

    cb = lax.dot_general(cm_b, bm_b, (((1,), (1,)), ((), ())), preferred_element_type=F32)

    def head_diag(e, x_pair):
        seg = acum[:, e:e + 1] - acum_t[e:e + 1, :]
        decay = jnp.where(causal, jnp.exp(jnp.minimum(seg, 0.0)), 0.0)
        return jnp.dot((cb * decay).astype(BF16), x_pair, preferred_element_type=F32)

    first_half = lax.broadcasted_iota(jnp.int32, (ln, 2 * p), 1) < p
    parts = []
    for pair in range(gh // 2):
        x_pair = xdt_b[:, pair * 2 * p:(pair + 1) * 2 * p]
        parts.append(jnp.where(first_half, head_diag(2 * pair, x_pair), head_diag(2 * pair + 1, x_pair)))
    y_diag = jnp.concatenate(parts, axis=1)

    state = state_ref[...]
    y_off = jnp.dot(cm_b, state.astype(BF16), preferred_element_type=F32) * jnp.exp(acum_x)
    to_end = jnp.exp(last_x - acum_x)
    upd = jnp.dot(bm.T.astype(BF16), (xdt * to_end).astype(BF16), preferred_element_type=F32)
    state_ref[...] = state * jnp.exp(last_x) + upd

    y = y_diag + y_off + dskip_ref[...] * xs
    zf = z_ref[...].astype(F32)
    y = y * (zf * _sigmoid(zf))
    y = y * lax.rsqrt(jnp.mean(y * y, axis=-1, keepdims=True) + RMS_EPS)
    o_ref[...] = (y * nw_ref[...]).astype(o_ref.dtype)


def _ssd_mixer(h, dt_raw, conv_w, conv_b, dt_bias, a_log, d_skip, norm_w, batch, seq,
               z_block, x_block, b_block, c_block):
    ln = min(SSD_CHUNK, seq)
    nc = seq // ln
    hb = ln // HALO
    gd, gh = SSD_GROUP_DIM, SSD_GROUP_HEADS
    ngb = D_SSD // SSD_STATE

    def cur(base):
        return lambda b, g, c: (b * nc + c, base + g)

    def halo(base):
        return lambda b, g, c: (jnp.maximum((b * nc + c) * hb - 1, 0), base + g)

    grp = lambda b, g, c: (g, 0, 0)
    in_specs = [
        pl.BlockSpec((ln, gd), cur(z_block)),
        pl.BlockSpec((ln, gd), cur(x_block)), pl.BlockSpec((HALO, gd), halo(x_block)),
        pl.BlockSpec((ln, SSD_STATE), cur(b_block)), pl.BlockSpec((HALO, SSD_STATE), halo(b_block)),
        pl.BlockSpec((ln, SSD_STATE), cur(c_block)), pl.BlockSpec((HALO, SSD_STATE), halo(c_block)),
        pl.BlockSpec((None, ln, V7X_LANES), lambda b, g, c: (g, b * nc + c, 0)),
        pl.BlockSpec((SSD_CONV, gd), lambda b, g, c: (0, g)), pl.BlockSpec((1, gd), lambda b, g, c: (0, g)),
        pl.BlockSpec((SSD_CONV, SSD_STATE), lambda b, g, c: (0, ngb + g)),
        pl.BlockSpec((1, SSD_STATE), lambda b, g, c: (0, ngb + g)),
        pl.BlockSpec((SSD_CONV, SSD_STATE), lambda b, g, c: (0, ngb + SSD_GROUPS + g)),
        pl.BlockSpec((1, SSD_STATE), lambda b, g, c: (0, ngb + SSD_GROUPS + g)),
        pl.BlockSpec((None, 1, V7X_LANES), grp), pl.BlockSpec((None, 1, V7X_LANES), grp),
        pl.BlockSpec((1, gd), lambda b, g, c: (0, g)), pl.BlockSpec((1, gd), lambda b, g, c: (0, g)),
    ]
    conv_b2 = conv_b.reshape(1, -1)
    d_skip_x = jnp.repeat(d_skip.astype(F32), SSD_HEADDIM).reshape(1, D_SSD)

    def per_head(v):
        return jnp.pad(v.astype(F32).reshape(SSD_GROUPS, 1, gh), ((0, 0), (0, 0), (0, V7X_LANES - gh)))
    return pl.pallas_call(
        functools.partial(_ssd_body, ln=ln),
        grid=(batch, SSD_GROUPS, nc),
        in_specs=in_specs,
        out_specs=pl.BlockSpec((ln, gd), lambda b, g, c: (b * nc + c, g)),
        out_shape=jax.ShapeDtypeStruct((batch * seq, D_SSD), BF16),
        scratch_shapes=[pltpu.VMEM((HALO + ln, gd), F32), pltpu.VMEM((SSD_STATE, gd), F32)],
        compiler_params=_params(("parallel", "parallel", "arbitrary")),
        name="ssd_mixer",
    )(h, h, h, h, h, h, h, dt_raw,
      conv_w, conv_b2, conv_w, conv_b2, conv_w, conv_b2,
      per_head(dt_bias), per_head(a_log), d_skip_x, norm_w.reshape(1, D_SSD))


def _gconv_body(bg_ref, cg_ref, cgh_ref, xi_ref, xih_ref, w_ref, o_ref, ext_ref, *, ts):
    i = pl.program_id(1)
    keep = jnp.where(i > 0, 1.0, 0.0).astype(F32)
    ext_ref[0:HALO, :] = cgh_ref[...].astype(F32) * xih_ref[...].astype(F32) * keep
    ext_ref[HALO:HALO + ts, :] = cg_ref[...].astype(F32) * xi_ref[...].astype(F32)
    acc = jnp.zeros((ts, D_CONV), F32)
    for tap in range(CONV_WIDTH):
        off = HALO - (CONV_WIDTH - 1) + tap
        acc = acc + w_ref[tap:tap + 1, :] * ext_ref[off:off + ts, :]
    o_ref[...] = (bg_ref[...].astype(F32) * acc).astype(o_ref.dtype)


def _gated_conv(h, conv_w, batch, seq):
    ts = min(CONV_TILE, seq)
    nt = seq // ts
    hb = ts // HALO

    def cur(blk):
        return lambda b, i: (b * nt + i, blk)

    def halo(blk):
        return lambda b, i: (jnp.maximum((b * nt + i) * hb - 1, 0), blk)

    return pl.pallas_call(
        functools.partial(_gconv_body, ts=ts),
        grid=(batch, nt),
        in_specs=[pl.BlockSpec((ts, D_CONV), cur(0)),
                  pl.BlockSpec((ts, D_CONV), cur(1)), pl.BlockSpec((HALO, D_CONV), halo(1)),
                  pl.BlockSpec((ts, D_CONV), cur(2)), pl.BlockSpec((HALO, D_CONV), halo(2)),
                  pl.BlockSpec((CONV_WIDTH, D_CONV), lambda b, i: (0, 0))],
        out_specs=pl.BlockSpec((ts, D_CONV), lambda b, i: (b * nt + i, 0)),
        out_shape=jax.ShapeDtypeStruct((batch * seq, D_CONV), BF16),
        scratch_shapes=[pltpu.VMEM((HALO + ts, D_CONV), F32)],
        compiler_params=_params(("parallel", "parallel")),
        name="gated_conv",
    )(h, h, h, h, h, conv_w)


def _sb_block(q, k_ref, v_ref, start, tk, carry, acc, upper, mask):
    kt = k_ref[pl.ds(start, tk), :]
    vt = v_ref[pl.ds(start, tk), :]
    z = lax.dot_general(q, kt, (((1,), (1,)), ((), ())), preferred_element_type=F32) * (SB_HEADDIM ** -0.5)
    sp = _softplus(z)
    log_stay = -sp if mask is None else jnp.where(mask, -sp, 0.0)
    hi = log_stay.astype(BF16)
    lo = (log_stay - hi.astype(F32)).astype(BF16)
    after = (jnp.dot(hi, upper, preferred_element_type=F32)
             + jnp.dot(lo, upper, preferred_element_type=F32)) + carry
    w = jnp.exp((z - sp) + after)
    if mask is not None:
        w = jnp.where(mask, w, 0.0)
    acc = acc + jnp.dot(w.astype(BF16), vt, preferred_element_type=F32)
    carry = carry + jnp.sum(log_stay, axis=1, keepdims=True)
    return carry, acc


def _sb_body(q_ref, k_ref, v_ref, o_ref, *, tq):
    qi = pl.program_id(2)
    q = q_ref[...]
    row = lax.broadcasted_iota(jnp.int32, (tq, tq), 0)
    col = lax.broadcasted_iota(jnp.int32, (tq, tq), 1)
    upper = jnp.where(row > col, 1.0, 0.0).astype(BF16)
    diag_mask = col < row

    def tile(j, carry, acc, mask):
        return _sb_block(q, k_ref, v_ref, pl.multiple_of(j * tq, tq), tq, carry, acc, upper, mask)

    def tile_pair(j, carry, acc):
        carry, acc = tile(j, carry, acc, None)
        has_second = jnp.broadcast_to(j >= 1, (tq, tq))
        return tile(jnp.maximum(j - 1, 0), carry, acc, has_second)

    carry = jnp.zeros((tq, 1), F32)
    acc = jnp.zeros((tq, SB_HEADDIM), F32)
    carry, acc = tile(qi, carry, acc, diag_mask)
    has_prev = jnp.broadcast_to(qi >= 1, (tq, tq))
    carry, acc = tile(jnp.maximum(qi - 1, 0), carry, acc, has_prev)

    def more(state):
        j, carry, _ = state
        return jnp.logical_and(j >= 0, jnp.max(carry) > SB_UNDERFLOW_LOG)

    def step(state):
        j, carry, acc = state
        carry, acc = tile_pair(j, carry, acc)
        return j - 2, carry, acc

    _, _, acc = lax.while_loop(more, step, (qi - 2, carry, acc))
    o_ref[...] = acc.astype(o_ref.dtype)


def _sb_attention(h, q_block, k_block, v_block, batch, seq):
    tq = min(SB_TILE, seq)
    nq = seq // tq
    return pl.pallas_call(
        functools.partial(_sb_body, tq=tq),
        grid=(batch, SB_HEADS, nq),
        in_specs=[pl.BlockSpec((tq, SB_HEADDIM), lambda b, hd, i: (b * nq + i, q_block + hd)),
                  pl.BlockSpec((seq, SB_HEADDIM), lambda b, hd, i: (b, k_block + hd)),
                  pl.BlockSpec((seq, SB_HEADDIM), lambda b, hd, i: (b, v_block + hd))],
        out_specs=pl.BlockSpec((tq, SB_HEADDIM), lambda b, hd, i: (b * nq + i, hd)),
        out_shape=jax.ShapeDtypeStruct((batch * seq, D_SB), BF16),
        compiler_params=_params(("parallel", "parallel", "arbitrary")),
        name="stick_breaking_attention",
    )(h, h, h)


def _router_body(x_ref, wt_ref, b_ref, idx_ref, wgt_ref):
    logits = lax.dot_general(wt_ref[...], x_ref[...], (((1,), (1,)), ((), ())),
                             precision=lax.Precision.HIGHEST, preferred_element_type=F32) + b_ref[...]
    iota = lax.broadcasted_iota(jnp.int32, logits.shape, 0).astype(F32)
    vals, idxs = [], []
    for _ in range(TOP_K):
        mx = jnp.max(logits, axis=0, keepdims=True)
        ix = jnp.min(jnp.where(logits == mx, iota, float(N_EXPERTS)), axis=0, keepdims=True)
        vals.append(mx)
        idxs.append(ix)
        logits = jnp.where(iota == ix, -jnp.inf, logits)
    exps = [jnp.exp(v - vals[0]) for v in vals]
    denom = exps[0] + exps[1] + exps[2] + exps[3]
    for k in range(TOP_K):
        idx_ref[k:k + 1, :] = idxs[k].astype(jnp.int32)
        wgt_ref[k:k + 1, :] = exps[k] / denom


def _router(x, router_w, router_b):
    t, d = x.shape
    tm = min(ROUTER_TILE, t)
    out = pl.BlockSpec((TOP_K, tm), lambda i: (0, i))
    return pl.pallas_call(
        _router_body,
        grid=(t // tm,),
        in_specs=[pl.BlockSpec((tm, d), lambda i: (i, 0)),
                  pl.BlockSpec((N_EXPERTS, d), lambda i: (0, 0)),
                  pl.BlockSpec((N_EXPERTS, 1), lambda i: (0, 0))],
        out_specs=[out, out],
        out_shape=[jax.ShapeDtypeStruct((TOP_K, t), jnp.int32), jax.ShapeDtypeStruct((TOP_K, t), F32)],
        compiler_params=_params(("parallel",)),
        name="moe_router",
    )(x, router_w.T, router_b.reshape(N_EXPERTS, 1))


def _routing_tables(idx, tm):
    t = idx.shape[1]
    pairs = TOP_K * t
    rows = pairs + N_EXPERTS * tm
    ntiles = rows // tm
    e_flat = idx.reshape(pairs)
    onehot = (e_flat[:, None] == jnp.arange(N_EXPERTS, dtype=jnp.int32)[None, :]).astype(jnp.int32)
    csum = jnp.cumsum(onehot, axis=0)
    rank = jnp.sum((csum - onehot) * onehot, axis=1)
    counts = csum[-1]
    padded = ((counts + tm - 1) // tm) * tm
    gend = jnp.cumsum(padded)
    gstart = gend - padded
    pos = (gstart[e_flat] + rank).astype(jnp.int32)
    token = (jnp.arange(pairs, dtype=jnp.int32) % t)
    tok_of_row = jnp.zeros((rows,), jnp.int32).at[pos].set(token)
    tile_start = jnp.arange(ntiles, dtype=jnp.int32) * tm
    tile_expert = jnp.minimum(jnp.searchsorted(gend, tile_start, side="right"), N_EXPERTS - 1).astype(jnp.int32)
    tile_valid = (tile_start < gend[-1]).astype(jnp.int32)
    pos_tok = pos.reshape(TOP_K, t).T.reshape(pairs)
    return tok_of_row, pos_tok, tile_expert, tile_valid


def _experts_body(te_ref, tv_ref, tok_ref, x_hbm, w1_ref, b1_ref, w2_ref, b2_ref, o_ref,
                  xbuf, sem, *, tm):
    i = pl.program_id(0)
    nt = pl.num_programs(0)
    half = x_hbm.shape[1]

    def row_copy(tok, slot, r):
        return pltpu.make_async_copy(x_hbm.at[pl.ds(tok, 1), :], xbuf.at[slot, pl.ds(r, 1), :], sem.at[slot])

    def issue(tile, slot):
        base = tile * tm

        def body(r, carry):
            row_copy(tok_ref[base + r], slot, r).start()
            return carry

        lax.fori_loop(0, tm, body, 0, unroll=MOE_ISSUE_UNROLL)

    @pl.when(i == 0)
    def _():
        issue(0, 0)

    @pl.when(i + 1 < nt)
    def _():
        issue(i + 1, (i + 1) % 2)

    slot = i % 2
    pltpu.make_async_copy(x_hbm.at[pl.ds(0, tm), :], xbuf.at[slot], sem.at[slot]).wait()

    @pl.when(tv_ref[i] == 1)
    def _():
        low, high = _unpack_halves(xbuf[slot])
        hh = (jnp.dot(low.astype(BF16), w1_ref[0:half, :], preferred_element_type=F32)
              + jnp.dot(high.astype(BF16), w1_ref[half:2 * half, :], preferred_element_type=F32)) + b1_ref[...]
        glu = jnp.minimum(hh[:, :D_EXPERT], SWIGLU_LIMIT)
        lin = jnp.clip(hh[:, D_EXPERT:], -SWIGLU_LIMIT, SWIGLU_LIMIT)
        act = glu * _sigmoid(SWIGLU_ALPHA * glu) * (lin + 1.0)
        out = jnp.dot(act.astype(BF16), w2_ref[...], preferred_element_type=F32) + b2_ref[...]
        o_ref[...] = _pack_halves(out)

    @pl.when(tv_ref[i] == 0)
    def _():
        o_ref[...] = jnp.zeros_like(o_ref)


def _experts(x_packed, w1, b1, w2, b2, tok_of_row, tile_expert, tile_valid, tm):
    t, half = x_packed.shape
    d = 2 * half
    rows = tok_of_row.shape[0]
    ntiles = rows // tm
    grid_spec = pltpu.PrefetchScalarGridSpec(
        num_scalar_prefetch=3,
        grid=(ntiles,),
        in_specs=[pl.BlockSpec(memory_space=pl.ANY),
                  pl.BlockSpec((None, d, 2 * D_EXPERT), lambda i, te, tv, tok: (te[i], 0, 0)),
                  pl.BlockSpec((None, 1, 2 * D_EXPERT), lambda i, te, tv, tok: (te[i], 0, 0)),
                  pl.BlockSpec((None, D_EXPERT, d), lambda i, te, tv, tok: (te[i], 0, 0)),
                  pl.BlockSpec((None, 1, d), lambda i, te, tv, tok: (te[i], 0, 0))],
        out_specs=pl.BlockSpec((tm, half), lambda i, te, tv, tok: (i, 0)),
        scratch_shapes=[pltpu.VMEM((2, tm, half), jnp.uint32), pltpu.SemaphoreType.DMA((2,))],
    )
    return pl.pallas_call(
        functools.partial(_experts_body, tm=tm),
        grid_spec=grid_spec,
        out_shape=jax.ShapeDtypeStruct((rows, half), jnp.uint32),
        compiler_params=_params(("arbitrary",)),
        name="moe_experts",
    )(tile_expert, tile_valid, tok_of_row, x_packed, w1, b1.reshape(N_EXPERTS, 1, -1), w2,
      b2.reshape(N_EXPERTS, 1, -1))


def _combine_body(pos_ref, y_hbm, x_ref, wgt_ref, g_ref, b_ref, of_ref, *rest, tm, emit_bf16):
    ob_ref = rest[0] if emit_bf16 else None
    ybuf, sem = rest[-2:]
    i = pl.program_id(0)
    nt = pl.num_programs(0)

    def row_copy(src, slot, k, r):
        return pltpu.make_async_copy(y_hbm.at[pl.ds(src, 1), :], ybuf.at[slot, k, pl.ds(r, 1), :], sem.at[slot])

    def issue(tile, slot):
        base = tile * tm * TOP_K

        def body(r, carry):
            for k in range(TOP_K):
                row_copy(pos_ref[base + r * TOP_K + k], slot, k, r).start()
            return carry

        lax.fori_loop(0, tm, body, 0, unroll=4)

    @pl.when(i == 0)
    def _():
        issue(0, 0)

    @pl.when(i + 1 < nt)
    def _():
        issue(i + 1, (i + 1) % 2)

    slot = i % 2
    for k in range(TOP_K):
        pltpu.make_async_copy(y_hbm.at[pl.ds(0, tm), :], ybuf.at[slot, k], sem.at[slot]).wait()
    wgt = wgt_ref[...]
    moe_low, moe_high = None, None
    for k in range(TOP_K):
        low, high = _unpack_halves(ybuf[slot, k])
        gk = wgt[:, k:k + 1]
        moe_low = gk * low if k == 0 else moe_low + gk * low
        moe_high = gk * high if k == 0 else moe_high + gk * high
    moe = jnp.concatenate([moe_low, moe_high], axis=1)
    y = _layer_norm_rows(DEEPNORM_ALPHA * x_ref[...] + moe, g_ref[...], b_ref[...])
    of_ref[...] = y
    if emit_bf16:
        ob_ref[...] = y.astype(BF16)


def _combine_layer_norm(x, y_rows, pos_tok, wgt_tok, g, b, tm, emit_bf16):
    t, d = x.shape
    half = d // 2
    row = lambda i, pos: (i, 0)
    vec = lambda i, pos: (0, 0)
    out_specs = [pl.BlockSpec((tm, d), row)]
    out_shape = [jax.ShapeDtypeStruct((t, d), F32)]
    if emit_bf16:
        out_specs.append(pl.BlockSpec((tm, d), row))
        out_shape.append(jax.ShapeDtypeStruct((t, d), BF16))
    grid_spec = pltpu.PrefetchScalarGridSpec(
        num_scalar_prefetch=1,
        grid=(t // tm,),
        in_specs=[pl.BlockSpec(memory_space=pl.ANY),
                  pl.BlockSpec((tm, d), row), pl.BlockSpec((tm, TOP_K), row),
                  pl.BlockSpec((1, d), vec), pl.BlockSpec((1, d), vec)],
        out_specs=out_specs,
        scratch_shapes=[pltpu.VMEM((2, TOP_K, tm, half), jnp.uint32), pltpu.SemaphoreType.DMA((2,))],
    )
    return pl.pallas_call(
        functools.partial(_combine_body, tm=tm, emit_bf16=emit_bf16),
        grid_spec=grid_spec,
        out_shape=out_shape,
        compiler_params=_params(("arbitrary",)),
        name="moe_combine_layer_norm",
    )(pos_tok, y_rows, x, wgt_tok, g.reshape(1, d), b.reshape(1, d))


def _moe_block(x_f32, x_packed, router_w, router_b, w1, b1, w2, b2, g, b, emit_bf16):
    t = x_f32.shape[0]
    tm = min(MOE_TILE, t)
    idx, wgt = _router(x_f32, router_w, router_b)
    tok_of_row, pos_tok, tile_expert, tile_valid = _routing_tables(idx, tm)
    y_rows = _experts(x_packed, w1.astype(BF16), b1, w2.astype(BF16), b2, tok_of_row, tile_expert, tile_valid, tm)
    return _combine_layer_norm(x_f32, y_rows, pos_tok, wgt.T, g, b, min(COMBINE_TILE, t), emit_bf16)


def _layer0_mixer(x_f32, x_b16, batch, seq, w_in, conv_w, conv_b, dt_bias, a_log, d_skip, norm_w,
                  pool_w, pool_scale, w_out, ln_g, ln_b):
    n_ssd = 2 * D_SSD + 2 * SSD_BC
    h_u = _matmul(x_b16, w_in, BF16, 1024, 512, 0, D_POOL)
    h = _matmul(x_b16, w_in, BF16, 1024, 512, D_POOL, n_ssd)
    w_dt = jnp.pad(w_in[:, D_POOL + n_ssd:], ((0, 0), (0, V7X_LANES - SSD_HEADS)))
    dt_raw = _matmul(x_b16, w_dt, F32, 1024, V7X_LANES)[:, :SSD_HEADS]
    t = batch * seq
    dt_raw = dt_raw.reshape(t, SSD_GROUPS, SSD_GROUP_HEADS).transpose(1, 0, 2)
    dt_raw = jnp.pad(dt_raw, ((0, 0), (0, 0), (0, V7X_LANES - SSD_GROUP_HEADS)))
    y_ssd = _ssd_mixer(h, dt_raw, conv_w, conv_b, dt_bias, a_log, d_skip, norm_w, batch, seq,
                       z_block=0, x_block=D_SSD // SSD_GROUP_DIM,
                       b_block=2 * D_SSD // SSD_STATE, c_block=(2 * D_SSD + SSD_BC) // SSD_STATE)
    y_pool = _pool_mixer(h_u, 0, pool_w.astype(BF16), pool_scale, batch, seq)
    m = _matmul_cat(y_pool, y_ssd, w_out, F32, 1024, 1024, D_POOL)
    return _residual_layer_norm(x_f32, m, ln_g, ln_b)


def _layer1_mixer(x_f32, x_b16, batch, seq, w_in, conv_w, w_out, ln_g, ln_b):
    h = _matmul(x_b16, w_in, BF16, 1024, 512)
    y_conv = _gated_conv(h, conv_w, batch, seq)
    qb = 3 * D_CONV // SB_HEADDIM
    y_sb = _sb_attention(h, qb, qb + SB_HEADS, qb + 2 * SB_HEADS, batch, seq)
    m = _matmul_cat(y_conv, y_sb, w_out, F32, 1024, 1024, D_CONV)
    return _residual_layer_norm(x_f32, m, ln_g, ln_b)


def kernel(x, l0_w_in, l0_conv_w, l0_conv_b, l0_dt_bias, l0_a_log, l0_d_skip, l0_ssm_norm_w, l0_pool_w, l0_pool_scale, l0_w_out, l0_ln_mix_g, l0_ln_mix_b, l0_router_w, l0_router_b, l0_w1, l0_b1, l0_w2, l0_b2, l0_ln_ffn_g, l0_ln_ffn_b, l1_w_in, l1_conv_w, l1_w_out, l1_ln_mix_g, l1_ln_mix_b, l1_router_w, l1_router_b, l1_w1, l1_b1, l1_w2, l1_b2, l1_ln_ffn_g, l1_ln_ffn_b):
    batch, seq, d = x.shape
    xf = x.reshape(batch * seq, d)
    xb = xf.astype(BF16)
    xf, xp = _layer0_mixer(xf, xb, batch, seq, l0_w_in, l0_conv_w, l0_conv_b, l0_dt_bias, l0_a_log, l0_d_skip,
                           l0_ssm_norm_w, l0_pool_w, l0_pool_scale, l0_w_out, l0_ln_mix_g, l0_ln_mix_b)
    xf, xb = _moe_block(xf, xp, l0_router_w, l0_router_b, l0_w1, l0_b1, l0_w2, l0_b2, l0_ln_ffn_g, l0_ln_ffn_b,
                        emit_bf16=True)
    xf, xp = _layer1_mixer(xf, xb, batch, seq, l1_w_in, l1_conv_w, l1_w_out, l1_ln_mix_g, l1_ln_mix_b)
    (xf,) = _moe_block(xf, xp, l1_router_w, l1_router_b, l1_w1, l1_b1, l1_w2, l1_b2, l1_ln_ffn_g, l1_ln_ffn_b,
                       emit_bf16=False)
    return xf.reshape(batch, seq, d)
```

```python
import functools

import jax
import jax.numpy as jnp
from jax import lax
from jax.experimental import pallas as pl
from jax.experimental.pallas import tpu as pltpu

F32 = jnp.float32
BF16 = jnp.bfloat16

D_MODEL = 4096
DEPTH = 2
DEEPNORM_ALPHA = (2.0 * DEPTH) ** 0.25
LN_EPS = 1e-5
RMS_EPS = 1e-5

POOL_WINDOWS = (2, 4, 8, 16)
POOL_GROUPS = 4
D_POOL = D_MODEL // 2
POOL_GROUP_DIM = D_POOL // POOL_GROUPS

SSD_HEADDIM = 64
D_SSD = (3 * D_MODEL) // 2
SSD_HEADS = D_SSD // SSD_HEADDIM
SSD_GROUPS = 8
SSD_STATE = 128
SSD_CONV = 4
SSD_GROUP_HEADS = SSD_HEADS // SSD_GROUPS
SSD_GROUP_DIM = D_SSD // SSD_GROUPS
SSD_BC = SSD_GROUPS * SSD_STATE

D_CONV = D_MODEL // 2
CONV_WIDTH = 3
SB_HEADS = 16
SB_HEADDIM = 128
D_SB = SB_HEADS * SB_HEADDIM

N_EXPERTS = 32
TOP_K = 4
D_EXPERT = 512
SWIGLU_LIMIT = 7.0
SWIGLU_ALPHA = 1.702

V7X_LANES = 128
V7X_BF16_SUBLANES = 16
V7X_VMEM_LIMIT = 56 * 1024 * 1024
HALO = V7X_BF16_SUBLANES

SSD_CHUNK = 256
POOL_TILE = 512
CONV_TILE = 512
LN_TILE = 256
ROUTER_TILE = 512
SB_TILE = 256
SB_UNDERFLOW_LOG = -110.0
MOE_TILE = 512
MOE_ISSUE_UNROLL = 8
COMBINE_TILE = 128


def _params(semantics):
    return pltpu.CompilerParams(dimension_semantics=semantics, vmem_limit_bytes=V7X_VMEM_LIMIT)


def _softplus(x):
    return jnp.maximum(x, 0.0) + jnp.log(1.0 + jnp.exp(-jnp.abs(x)))


def _sigmoid(x):
    return 1.0 / (1.0 + jnp.exp(-x))


def _bf16_terms(v, terms):
    out = []
    for _ in range(terms - 1):
        piece = v.astype(BF16)
        out.append(piece)
        v = v - piece.astype(F32)
    out.append(v.astype(BF16))
    return out


def _dot_with_01(v, mat01, terms, v_on_left=True):
    parts = _bf16_terms(v, terms)
    if v_on_left:
        dots = [jnp.dot(p, mat01, preferred_element_type=F32) for p in parts]
    else:
        dots = [jnp.dot(mat01, p, preferred_element_type=F32) for p in parts]
    total = dots[0]
    for d in dots[1:]:
        total = total + d
    return total


def _mm_body(a_ref, w_ref, o_ref, wb_ref):
    @pl.when(pl.program_id(1) == 0)
    def _():
        wb_ref[...] = w_ref[...].astype(BF16)

    o_ref[...] = jnp.dot(a_ref[...], wb_ref[...], preferred_element_type=F32).astype(o_ref.dtype)


def _matmul(a, w, out_dtype, tm, tn, col0=0, ncols=None):
    m, kdim = a.shape
    n = w.shape[1] - col0 if ncols is None else ncols
    tm, tn = min(tm, m), min(tn, n)
    assert m % tm == 0 and n % tn == 0 and col0 % tn == 0 and w.shape[0] == kdim
    cb0 = col0 // tn
    return pl.pallas_call(
        _mm_body,
        grid=(n // tn, m // tm),
        in_specs=[pl.BlockSpec((tm, kdim), lambda j, i: (i, 0)),
                  pl.BlockSpec((kdim, tn), lambda j, i: (0, cb0 + j))],
        out_specs=pl.BlockSpec((tm, tn), lambda j, i: (i, j)),
        out_shape=jax.ShapeDtypeStruct((m, n), out_dtype),
        scratch_shapes=[pltpu.VMEM((kdim, tn), BF16)],
        compiler_params=_params(("parallel", "arbitrary")),
        name="matmul",
    )(a, w)


def _mm_cat_body(a1_ref, a2_ref, w_ref, o_ref, acc_ref, *, n1, nk):
    k = pl.program_id(2)

    def accumulate(a_ref):
        part = jnp.dot(a_ref[...], w_ref[...].astype(BF16), preferred_element_type=F32)

        @pl.when(k == 0)
        def _():
            acc_ref[...] = part

        @pl.when(k > 0)
        def _():
            acc_ref[...] += part

    @pl.when(k < n1)
    def _():
        accumulate(a1_ref)

    @pl.when(k >= n1)
    def _():
        accumulate(a2_ref)

    @pl.when(k == nk - 1)
    def _():
        o_ref[...] = acc_ref[...].astype(o_ref.dtype)


def _matmul_cat(a1, a2, w, out_dtype, tm, tn, tk):
    m, k1 = a1.shape
    k2 = a2.shape[1]
    n = w.shape[1]
    assert k1 % tk == 0 and k2 % tk == 0 and m % tm == 0 and n % tn == 0 and w.shape[0] == k1 + k2
    n1, nk = k1 // tk, (k1 + k2) // tk
    return pl.pallas_call(
        functools.partial(_mm_cat_body, n1=n1, nk=nk),
        grid=(n // tn, m // tm, nk),
        in_specs=[pl.BlockSpec((tm, tk), lambda j, i, k: (i, jnp.minimum(k, n1 - 1))),
                  pl.BlockSpec((tm, tk), lambda j, i, k: (i, jnp.maximum(k - n1, 0))),
                  pl.BlockSpec((tk, tn), lambda j, i, k: (k, j))],
        out_specs=pl.BlockSpec((tm, tn), lambda j, i, k: (i, j)),
        out_shape=jax.ShapeDtypeStruct((m, n), out_dtype),
        scratch_shapes=[pltpu.VMEM((tm, tn), F32)],
        compiler_params=_params(("parallel", "parallel", "arbitrary")),
        name="matmul_cat",
    )(a1, a2, w)


def _layer_norm_rows(v, g, b):
    mu = jnp.mean(v, axis=-1, keepdims=True)
    vc = v - mu
    var = jnp.mean(vc * vc, axis=-1, keepdims=True)
    return vc * lax.rsqrt(var + LN_EPS) * g + b


def _bf16_bits(v):
    return lax.bitcast_convert_type(v.astype(BF16).astype(F32), jnp.uint32)


def _pack_halves(v):
    n = v.shape[1] // 2
    return _bf16_bits(v[:, n:]) | (_bf16_bits(v[:, :n]) >> 16)


def _unpack_halves(words):
    low = lax.bitcast_convert_type(words << 16, F32)
    high = lax.bitcast_convert_type(words & jnp.uint32(0xFFFF0000), F32)
    return low, high


def _ln_body(x_ref, m_ref, g_ref, b_ref, of_ref, op_ref):
    v = DEEPNORM_ALPHA * x_ref[...] + m_ref[...].astype(F32)
    y = _layer_norm_rows(v, g_ref[...], b_ref[...])
    of_ref[...] = y
    op_ref[...] = _pack_halves(y)


def _residual_layer_norm(x, m, g, b):
    t, d = x.shape
    tm = min(LN_TILE, t)
    row = pl.BlockSpec((tm, d), lambda i: (i, 0))
    half = pl.BlockSpec((tm, d // 2), lambda i: (i, 0))
    vec = pl.BlockSpec((1, d), lambda i: (0, 0))
    return pl.pallas_call(
        _ln_body,
        grid=(t // tm,),
        in_specs=[row, row, vec, vec],
        out_specs=[row, half],
        out_shape=[jax.ShapeDtypeStruct((t, d), F32), jax.ShapeDtypeStruct((t, d // 2), jnp.uint32)],
        compiler_params=_params(("parallel",)),
        name="residual_layer_norm",
    )(x, m, g.reshape(1, d), b.reshape(1, d))


def _pool_body(u_ref, halo_ref, w_ref, scale_ref, o_ref, ext_ref, *, ts):
    i = pl.program_id(1)
    keep = jnp.where(i > 0, 1.0, 0.0).astype(F32)
    pos = (i * ts + lax.broadcasted_iota(jnp.int32, (ts, 1), 0) + 1).astype(F32)
    for g, win in enumerate(POOL_WINDOWS):
        cols = slice(g * POOL_GROUP_DIM, (g + 1) * POOL_GROUP_DIM)
        cur = u_ref[:, cols].astype(F32)
        ext_ref[0:HALO, :] = halo_ref[:, cols].astype(F32) * keep
        ext_ref[HALO:HALO + ts, :] = cur
        acc = cur
        for back in range(1, win):
            acc = acc + ext_ref[HALO - back:HALO - back + ts, :]
        mean = acc / jnp.minimum(pos, float(win))
        mixed = jnp.dot((mean - cur).astype(BF16), w_ref[g], preferred_element_type=F32)
        o_ref[:, cols] = (mixed * scale_ref[:, cols]).astype(o_ref.dtype)


def _pool_mixer(h, col_block, pool_w, pool_scale, batch, seq):
    ts = min(POOL_TILE, seq)
    nt = seq // ts
    hb = ts // HALO

    def cur_map(b, i):
        return (b * nt + i, col_block)

    def halo_map(b, i):
        return (jnp.maximum((b * nt + i) * hb - 1, 0), col_block)

    return pl.pallas_call(
        functools.partial(_pool_body, ts=ts),
        grid=(batch, nt),
        in_specs=[pl.BlockSpec((ts, D_POOL), cur_map),
                  pl.BlockSpec((HALO, D_POOL), halo_map),
                  pl.BlockSpec((POOL_GROUPS, POOL_GROUP_DIM, POOL_GROUP_DIM), lambda b, i: (0, 0, 0)),
                  pl.BlockSpec((1, D_POOL), lambda b, i: (0, 0))],
        out_specs=pl.BlockSpec((ts, D_POOL), lambda b, i: (b * nt + i, 0)),
        out_shape=jax.ShapeDtypeStruct((batch * seq, D_POOL), BF16),
        scratch_shapes=[pltpu.VMEM((HALO + ts, POOL_GROUP_DIM), F32)],
        compiler_params=_params(("parallel", "parallel")),
        name="pool_mixer",
    )(h, h, pool_w, pool_scale.reshape(1, D_POOL))


def _conv_silu(cur_ref, halo_ref, w_ref, b_ref, ext_ref, keep, width, ln):
    ext_ref[0:HALO, 0:width] = halo_ref[...].astype(F32) * keep
    ext_ref[HALO:HALO + ln, 0:width] = cur_ref[...].astype(F32)
    acc = b_ref[...]
    for tap in range(SSD_CONV):
        off = HALO - (SSD_CONV - 1) + tap
        acc = acc + w_ref[tap:tap + 1, :] * ext_ref[off:off + ln, 0:width]
    return acc * _sigmoid(acc)


def _ssd_body(z_ref, x_ref, xh_ref, b_ref, bh_ref, c_ref, ch_ref, dt_ref,
              cwx_ref, cbx_ref, cwb_ref, cbb_ref, cwc_ref, cbc_ref,
              dtb_ref, alog_ref, dskip_ref, nw_ref, o_ref, ext_ref, state_ref, *, ln):
    c = pl.program_id(2)
    keep = jnp.where(c > 0, 1.0, 0.0).astype(F32)

    @pl.when(c == 0)
    def _():
        state_ref[...] = jnp.zeros_like(state_ref)

    gh, p, gd = SSD_GROUP_HEADS, SSD_HEADDIM, SSD_GROUP_DIM

    xs = _conv_silu(x_ref, xh_ref, cwx_ref, cbx_ref, ext_ref, keep, gd, ln)
    bm = _conv_silu(b_ref, bh_ref, cwb_ref, cbb_ref, ext_ref, keep, SSD_STATE, ln)
    cm = _conv_silu(c_ref, ch_ref, cwc_ref, cbc_ref, ext_ref, keep, SSD_STATE, ln)

    dt = _softplus(dt_ref[...] + dtb_ref[...])
    da = dt * (-jnp.exp(alog_ref[...]))

    row = lax.broadcasted_iota(jnp.int32, (ln, ln), 0)
    col = lax.broadcasted_iota(jnp.int32, (ln, ln), 1)
    causal = row >= col
    tri = jnp.where(causal, 1.0, 0.0).astype(BF16)
    acum = _dot_with_01(da, tri, 3, v_on_left=False)
    acum_t = acum.T

    eh = lax.broadcasted_iota(jnp.int32, (V7X_LANES, gd), 0)
    ec = lax.broadcasted_iota(jnp.int32, (V7X_LANES, gd), 1)
    expand = jnp.where((ec >= eh * p) & (ec < (eh + 1) * p), 1.0, 0.0).astype(BF16)
    dt_x = _dot_with_01(dt, expand, 2)
    acum_x = _dot_with_01(acum, expand, 3)
    last_x = acum_x[ln - 1:ln, :]

    xdt = xs * dt_x
    xdt_b = xdt.astype(BF16)
    bm_b = bm.astype(BF16)
    cm_b = cm.astype(BF16)

    cb = lax.dot_general(cm_b, bm_b, (((1,), (1,)), ((), ())), preferred_element_type=F32)

    def head_diag(e, x_pair):
        seg = acum[:, e:e + 1] - acum_t[e:e + 1, :]
        decay = jnp.where(causal, jnp.exp(jnp.minimum(seg, 0.0)), 0.0)
        return jnp.dot((cb * decay).astype(BF16), x_pair, preferred_element_type=F32)

    first_half = lax.broadcasted_iota(jnp.int32, (ln, 2 * p), 1) < p
    parts = []
    for pair in range(gh // 2):
        x_pair = xdt_b[:, pair * 2 * p:(pair + 1) * 2 * p]
        parts.append(jnp.where(first_half, head_diag(2 * pair, x_pair), head_diag(2 * pair + 1, x_pair)))
    y_diag = jnp.concatenate(parts, axis=1)

    state = state_ref[...]
    y_off = jnp.dot(cm_b, state.astype(BF16), preferred_element_type=F32) * jnp.exp(acum_x)
    to_end = jnp.exp(last_x - acum_x)
    upd = jnp.dot(bm.T.astype(BF16), (xdt * to_end).astype(BF16), preferred_element_type=F32)
    state_ref[...] = state * jnp.exp(last_x) + upd

    y = y_diag + y_off + dskip_ref[...] * xs
    zf = z_ref[...].astype(F32)
    y = y * (zf * _sigmoid(zf))
    y = y * lax.rsqrt(jnp.mean(y * y, axis=-1, keepdims=True) + RMS_EPS)
    o_ref[...] = (y * nw_ref[...]).astype(o_ref.dtype)


def _ssd_mixer(h, dt_raw, conv_w, conv_b, dt_bias, a_log, d_skip, norm_w, batch, seq,
               z_block, x_block, b_block, c_block):
    ln = min(SSD_CHUNK, seq)
    nc = seq // ln
    hb = ln // HALO
    gd, gh = SSD_GROUP_DIM, SSD_GROUP_HEADS
    ngb = D_SSD // SSD_STATE

    def cur(base):
        return lambda b, g, c: (b * nc + c, base + g)

    def halo(base):
        return lambda b, g, c: (jnp.maximum((b * nc + c) * hb - 1, 0), base + g)

    grp = lambda b, g, c: (g, 0, 0)
    in_specs = [
        pl.BlockSpec((ln, gd), cur(z_block)),
        pl.BlockSpec((ln, gd), cur(x_block)), pl.BlockSpec((HALO, gd), halo(x_block)),
        pl.BlockSpec((ln, SSD_STATE), cur(b_block)), pl.BlockSpec((HALO, SSD_STATE), halo(b_block)),
        pl.BlockSpec((ln, SSD_STATE), cur(c_block)), pl.BlockSpec((HALO, SSD_STATE), halo(c_block)),
        pl.BlockSpec((None, ln, V7X_LANES), lambda b, g, c: (g, b * nc + c, 0)),
        pl.BlockSpec((SSD_CONV, gd), lambda b, g, c: (0, g)), pl.BlockSpec((1, gd), lambda b, g, c: (0, g)),
        pl.BlockSpec((SSD_CONV, SSD_STATE), lambda b, g, c: (0, ngb + g)),
        pl.BlockSpec((1, SSD_STATE), lambda b, g, c: (0, ngb + g)),
        pl.BlockSpec((SSD_CONV, SSD_STATE), lambda b, g, c: (0, ngb + SSD_GROUPS + g)),
        pl.BlockSpec((1, SSD_STATE), lambda b, g, c: (0, ngb + SSD_GROUPS + g)),
        pl.BlockSpec((None, 1, V7X_LANES), grp), pl.BlockSpec((None, 1, V7X_LANES), grp),
        pl.BlockSpec((1, gd), lambda b, g, c: (0, g)), pl.BlockSpec((1, gd), lambda b, g, c: (0, g)),
    ]
    conv_b2 = conv_b.reshape(1, -1)
    d_skip_x = jnp.repeat(d_skip.astype(F32), SSD_HEADDIM).reshape(1, D_SSD)

    def per_head(v):
        return jnp.pad(v.astype(F32).reshape(SSD_GROUPS, 1, gh), ((0, 0), (0, 0), (0, V7X_LANES - gh)))
    return pl.pallas_call(
        functools.partial(_ssd_body, ln=ln),
        grid=(batch, SSD_GROUPS, nc),
        in_specs=in_specs,
        out_specs=pl.BlockSpec((ln, gd), lambda b, g, c: (b * nc + c, g)),
        out_shape=jax.ShapeDtypeStruct((batch * seq, D_SSD), BF16),
        scratch_shapes=[pltpu.VMEM((HALO + ln, gd), F32), pltpu.VMEM((SSD_STATE, gd), F32)],
        compiler_params=_params(("parallel", "parallel", "arbitrary")),
        name="ssd_mixer",
    )(h, h, h, h, h, h, h, dt_raw,
      conv_w, conv_b2, conv_w, conv_b2, conv_w, conv_b2,
      per_head(dt_bias), per_head(a_log), d_skip_x, norm_w.reshape(1, D_SSD))


def _gconv_body(bg_ref, cg_ref, cgh_ref, xi_ref, xih_ref, w_ref, o_ref, ext_ref, *, ts):
    i = pl.program_id(1)
    keep = jnp.where(i > 0, 1.0, 0.0).astype(F32)
    ext_ref[0:HALO, :] = cgh_ref[...].astype(F32) * xih_ref[...].astype(F32) * keep
    ext_ref[HALO:HALO + ts, :] = cg_ref[...].astype(F32) * xi_ref[...].astype(F32)
    acc = jnp.zeros((ts, D_CONV), F32)
    for tap in range(CONV_WIDTH):
        off = HALO - (CONV_WIDTH - 1) + tap
        acc = acc + w_ref[tap:tap + 1, :] * ext_ref[off:off + ts, :]
    o_ref[...] = (bg_ref[...].astype(F32) * acc).astype(o_ref.dtype)


def _gated_conv(h, conv_w, batch, seq):
    ts = min(CONV_TILE, seq)
    nt = seq // ts
    hb = ts // HALO

    def cur(blk):
        return lambda b, i: (b * nt + i, blk)

    def halo(blk):
        return lambda b, i: (jnp.maximum((b * nt + i) * hb - 1, 0), blk)

    return pl.pallas_call(
        functools.partial(_gconv_body, ts=ts),
        grid=(batch, nt),
        in_specs=[pl.BlockSpec((ts, D_CONV), cur(0)),
                  pl.BlockSpec((ts, D_CONV), cur(1)), pl.BlockSpec((HALO, D_CONV), halo(1)),
                  pl.BlockSpec((ts, D_CONV), cur(2)), pl.BlockSpec((HALO, D_CONV), halo(2)),
                  pl.BlockSpec((CONV_WIDTH, D_CONV), lambda b, i: (0, 0))],
        out_specs=pl.BlockSpec((ts, D_CONV), lambda b, i: (b * nt + i, 0)),
        out_shape=jax.ShapeDtypeStruct((batch * seq, D_CONV), BF16),
        scratch_shapes=[pltpu.VMEM((HALO + ts, D_CONV), F32)],
        compiler_params=_params(("parallel", "parallel")),
        name="gated_conv",
    )(h, h, h, h, h, conv_w)


def _sb_block(q, k_ref, v_ref, start, tk, carry, acc, upper, mask):
    kt = k_ref[pl.ds(start, tk), :]
    vt = v_ref[pl.ds(start, tk), :]
    z = lax.dot_general(q, kt, (((1,), (1,)), ((), ())), preferred_element_type=F32) * (SB_HEADDIM ** -0.5)
    sp = _softplus(z)
    log_stay = -sp if mask is None else jnp.where(mask, -sp, 0.0)
    hi = log_stay.astype(BF16)
    lo = (log_stay - hi.astype(F32)).astype(BF16)
    after = (jnp.dot(hi, upper, preferred_element_type=F32)
             + jnp.dot(lo, upper, preferred_element_type=F32)) + carry
    w = jnp.exp((z - sp) + after)
    if mask is not None:
        w = jnp.where(mask, w, 0.0)
    acc = acc + jnp.dot(w.astype(BF16), vt, preferred_element_type=F32)
    carry = carry + jnp.sum(log_stay, axis=1, keepdims=True)
    return carry, acc


def _sb_body(q_ref, k_ref, v_ref, o_ref, *, tq):
    qi = pl.program_id(2)
    q = q_ref[...]
    row = lax.broadcasted_iota(jnp.int32, (tq, tq), 0)
    col = lax.broadcasted_iota(jnp.int32, (tq, tq), 1)
    upper = jnp.where(row > col, 1.0, 0.0).astype(BF16)
    diag_mask = col < row

    def tile(j, carry, acc, mask):
        return _sb_block(q, k_ref, v_ref, pl.multiple_of(j * tq, tq), tq, carry, acc, upper, mask)

    def tile_pair(j, carry, acc):
        carry, acc = tile(j, carry, acc, None)
        has_second = jnp.broadcast_to(j >= 1, (tq, tq))
        return tile(jnp.maximum(j - 1, 0), carry, acc, has_second)

    carry = jnp.zeros((tq, 1), F32)
    acc = jnp.zeros((tq, SB_HEADDIM), F32)
    carry, acc = tile(qi, carry, acc, diag_mask)
    has_prev = jnp.broadcast_to(qi >= 1, (tq, tq))
    carry, acc = tile(jnp.maximum(qi - 1, 0), carry, acc, has_prev)

    def more(state):
        j, carry, _ = state
        return jnp.logical_and(j >= 0, jnp.max(carry) > SB_UNDERFLOW_LOG)

    def step(state):
        j, carry, acc = state
        carry, acc = tile_pair(j, carry, acc)
        return j - 2, carry, acc

    _, _, acc = lax.while_loop(more, step, (qi - 2, carry, acc))
    o_ref[...] = acc.astype(o_ref.dtype)


def _sb_attention(h, q_block, k_block, v_block, batch, seq):
    tq = min(SB_TILE, seq)
    nq = seq // tq
    return pl.pallas_call(
        functools.partial(_sb_body, tq=tq),
        grid=(batch, SB_HEADS, nq),
        in_specs=[pl.BlockSpec((tq, SB_HEADDIM), lambda b, hd, i: (b * nq + i, q_block + hd)),
                  pl.BlockSpec((seq, SB_HEADDIM), lambda b, hd, i: (b, k_block + hd)),
                  pl.BlockSpec((seq, SB_HEADDIM), lambda b, hd, i: (b, v_block + hd))],
        out_specs=pl.BlockSpec((tq, SB_HEADDIM), lambda b, hd, i: (b * nq + i, hd)),
        out_shape=jax.ShapeDtypeStruct((batch * seq, D_SB), BF16),
        compiler_params=_params(("parallel", "parallel", "arbitrary")),
        name="stick_breaking_attention",
    )(h, h, h)


def _router_body(x_ref, wt_ref, b_ref, idx_ref, wgt_ref):
    w_hi, w_lo = _bf16_terms(wt_ref[...], 2)
    x_hi, x_lo = _bf16_terms(x_ref[...], 2)
    nt_dot = lambda a, b: lax.dot_general(a, b, (((1,), (1,)), ((), ())), preferred_element_type=F32)
    logits = (nt_dot(w_hi, x_hi) + (nt_dot(w_hi, x_lo) + nt_dot(w_lo, x_hi))) + b_ref[...]
    iota = lax.broadcasted_iota(jnp.int32, logits.shape, 0).astype(F32)
    vals, idxs = [], []
    for _ in range(TOP_K):
        mx = jnp.max(logits, axis=0, keepdims=True)
        ix = jnp.min(jnp.where(logits == mx, iota, float(N_EXPERTS)), axis=0, keepdims=True)
        vals.append(mx)
        idxs.append(ix)
        logits = jnp.where(iota == ix, -jnp.inf, logits)
    exps = [jnp.exp(v - vals[0]) for v in vals]
    denom = exps[0] + exps[1] + exps[2] + exps[3]
    for k in range(TOP_K):
        idx_ref[k:k + 1, :] = idxs[k].astype(jnp.int32)
        wgt_ref[k:k + 1, :] = exps[k] / denom


def _router(x, router_w, router_b):
    t, d = x.shape
    tm = min(ROUTER_TILE, t)
    out = pl.BlockSpec((TOP_K, tm), lambda i: (0, i))
    return pl.pallas_call(
        _router_body,
        grid=(t // tm,),
        in_specs=[pl.BlockSpec((tm, d), lambda i: (i, 0)),
                  pl.BlockSpec((N_EXPERTS, d), lambda i: (0, 0)),
                  pl.BlockSpec((N_EXPERTS, 1), lambda i: (0, 0))],
        out_specs=[out, out],
        out_shape=[jax.ShapeDtypeStruct((TOP_K, t), jnp.int32), jax.ShapeDtypeStruct((TOP_K, t), F32)],
        compiler_params=_params(("parallel",)),
        name="moe_router",
    )(x, router_w.T, router_b.reshape(N_EXPERTS, 1))


def _routing_tables(idx, tm):
    t = idx.shape[1]
    pairs = TOP_K * t
    rows = pairs + N_EXPERTS * tm
    ntiles = rows // tm
    e_flat = idx.reshape(pairs)
    onehot = (e_flat[:, None] == jnp.arange(N_EXPERTS, dtype=jnp.int32)[None, :]).astype(jnp.int32)
    csum = jnp.cumsum(onehot, axis=0)
    rank = jnp.sum((csum - onehot) * onehot, axis=1)
    counts = csum[-1]
    padded = ((counts + tm - 1) // tm) * tm
    gend = jnp.cumsum(padded)
    gstart = gend - padded
    pos = (gstart[e_flat] + rank).astype(jnp.int32)
    token = (jnp.arange(pairs, dtype=jnp.int32) % t)
    tok_of_row = jnp.zeros((rows,), jnp.int32).at[pos].set(token)
    tile_start = jnp.arange(ntiles, dtype=jnp.int32) * tm
    tile_expert = jnp.sum((tile_start[:, None] >= gend[None, :]).astype(jnp.int32), axis=1)
    tile_expert = jnp.minimum(tile_expert, N_EXPERTS - 1)
    tile_valid = (tile_start < gend[-1]).astype(jnp.int32)
    pos_tok = pos.reshape(TOP_K, t).T.reshape(pairs)
    return tok_of_row, pos_tok, tile_expert, tile_valid


def _dispatch_body(tok_ref, x_hbm, o_hbm, sem, *, batch):
    i = pl.program_id(0)
    nt = pl.num_programs(0)
    slot = i % 2
    base = i * batch

    def body(r, carry):
        pltpu.make_async_copy(x_hbm.at[pl.ds(tok_ref[base + r], 1), :], o_hbm.at[pl.ds(base + r, 1), :],
                              sem.at[slot]).start()
        return carry

    lax.fori_loop(0, batch, body, 0, unroll=MOE_ISSUE_UNROLL)

    def wait_batch(s):
        pltpu.make_async_copy(x_hbm.at[pl.ds(0, batch), :], o_hbm.at[pl.ds(0, batch), :], sem.at[s]).wait()

    @pl.when(i > 0)
    def _():
        wait_batch(1 - slot)

    @pl.when(i == nt - 1)
    def _():
        wait_batch(slot)


def _dispatch(x_packed, tok_of_row, batch):
    half = x_packed.shape[1]
    rows = tok_of_row.shape[0]
    grid_spec = pltpu.PrefetchScalarGridSpec(
        num_scalar_prefetch=1,
        grid=(rows // batch,),
        in_specs=[pl.BlockSpec(memory_space=pl.ANY)],
        out_specs=pl.BlockSpec(memory_space=pl.ANY),
        scratch_shapes=[pltpu.SemaphoreType.DMA((2,))],
    )
    return pl.pallas_call(
        functools.partial(_dispatch_body, batch=batch),
        grid_spec=grid_spec,
        out_shape=jax.ShapeDtypeStruct((rows, half), jnp.uint32),
        compiler_params=_params(("arbitrary",)),
        name="moe_dispatch",
    )(tok_of_row, x_packed)


def _expert_changed(te_ref, i):
    return jnp.logical_or(i == 0, te_ref[i] != te_ref[jnp.maximum(i - 1, 0)])


def _up_body(te_ref, tv_ref, x_ref, w1_ref, b1_ref, o_ref, w1b_ref):
    i = pl.program_id(0)
    half = x_ref.shape[1]

    @pl.when(_expert_changed(te_ref, i))
    def _():
        w1b_ref[...] = w1_ref[...].astype(BF16)

    @pl.when(tv_ref[i] == 1)
    def _():
        low, high = _unpack_halves(x_ref[...])
        hh = (jnp.dot(low.astype(BF16), w1b_ref[0:half, :], preferred_element_type=F32)
              + jnp.dot(high.astype(BF16), w1b_ref[half:2 * half, :], preferred_element_type=F32)) + b1_ref[...]
        glu = jnp.minimum(hh[:, :D_EXPERT], SWIGLU_LIMIT)
        lin = jnp.clip(hh[:, D_EXPERT:], -SWIGLU_LIMIT, SWIGLU_LIMIT)
        o_ref[...] = (glu * _sigmoid(SWIGLU_ALPHA * glu) * (lin + 1.0)).astype(o_ref.dtype)

    @pl.when(tv_ref[i] == 0)
    def _():
        o_ref[...] = jnp.zeros_like(o_ref)


def _down_body(te_ref, tv_ref, a_ref, w2_ref, b2_ref, o_ref, w2b_ref):
    i = pl.program_id(0)

    @pl.when(_expert_changed(te_ref, i))
    def _():
        w2b_ref[...] = w2_ref[...].astype(BF16)

    @pl.when(tv_ref[i] == 1)
    def _():
        out = jnp.dot(a_ref[...], w2b_ref[...], preferred_element_type=F32) + b2_ref[...]
        o_ref[...] = _pack_halves(out)

    @pl.when(tv_ref[i] == 0)
    def _():
        o_ref[...] = jnp.zeros_like(o_ref)


def _experts(x_sorted, w1, b1, w2, b2, tile_expert, tile_valid, tm):
    rows, half = x_sorted.shape
    d = 2 * half
    ntiles = rows // tm
    by_expert = lambda i, te, tv: (te[i], 0, 0)
    by_tile = lambda i, te, tv: (i, 0)
    act = pl.pallas_call(
        _up_body,
        grid_spec=pltpu.PrefetchScalarGridSpec(
            num_scalar_prefetch=2,
            grid=(ntiles,),
            in_specs=[pl.BlockSpec((tm, half), by_tile),
                      pl.BlockSpec((None, d, 2 * D_EXPERT), by_expert),
                      pl.BlockSpec((None, 1, 2 * D_EXPERT), by_expert)],
            out_specs=pl.BlockSpec((tm, D_EXPERT), by_tile),
            scratch_shapes=[pltpu.VMEM((d, 2 * D_EXPERT), BF16)],
        ),
        out_shape=jax.ShapeDtypeStruct((rows, D_EXPERT), BF16),
        compiler_params=_params(("arbitrary",)),
        name="moe_up",
    )(tile_expert, tile_valid, x_sorted, w1, b1.reshape(N_EXPERTS, 1, -1))
    return pl.pallas_call(
        _down_body,
        grid_spec=pltpu.PrefetchScalarGridSpec(
            num_scalar_prefetch=2,
            grid=(ntiles,),
            in_specs=[pl.BlockSpec((tm, D_EXPERT), by_tile),
                      pl.BlockSpec((None, D_EXPERT, d), by_expert),
                      pl.BlockSpec((None, 1, d), by_expert)],
            out_specs=pl.BlockSpec((tm, half), by_tile),
            scratch_shapes=[pltpu.VMEM((D_EXPERT, d), BF16)],
        ),
        out_shape=jax.ShapeDtypeStruct((rows, half), jnp.uint32),
        compiler_params=_params(("arbitrary",)),
        name="moe_down",
    )(tile_expert, tile_valid, act, w2, b2.reshape(N_EXPERTS, 1, -1))


def _combine_body(pos_ref, y_hbm, x_ref, wgt_ref, g_ref, b_ref, of_ref, *rest, tm, emit_bf16):
    ob_ref = rest[0] if emit_bf16 else None
    ybuf, sem = rest[-2:]
    i = pl.program_id(0)
    nt = pl.num_programs(0)

    def row_copy(src, slot, k, r):
        return pltpu.make_async_copy(y_hbm.at[pl.ds(src, 1), :], ybuf.at[slot, k, pl.ds(r, 1), :], sem.at[slot])

    def issue(tile, slot):
        base = tile * tm * TOP_K

        def body(r, carry):
            for k in range(TOP_K):
                row_copy(pos_ref[base + r * TOP_K + k], slot, k, r).start()
            return carry

        lax.fori_loop(0, tm, body, 0, unroll=4)

    @pl.when(i == 0)
    def _():
        issue(0, 0)

    @pl.when(i + 1 < nt)
    def _():
        issue(i + 1, (i + 1) % 2)

    slot = i % 2
    for k in range(TOP_K):
        pltpu.make_async_copy(y_hbm.at[pl.ds(0, tm), :], ybuf.at[slot, k], sem.at[slot]).wait()
    wgt = wgt_ref[...]
    moe_low, moe_high = None, None
    for k in range(TOP_K):
        low, high = _unpack_halves(ybuf[slot, k])
        gk = wgt[:, k:k + 1]
        moe_low = gk * low if k == 0 else moe_low + gk * low
        moe_high = gk * high if k == 0 else moe_high + gk * high
    moe = jnp.concatenate([moe_low, moe_high], axis=1)
    y = _layer_norm_rows(DEEPNORM_ALPHA * x_ref[...] + moe, g_ref[...], b_ref[...])
    of_ref[...] = y
    if emit_bf16:
        ob_ref[...] = y.astype(BF16)


def _combine_layer_norm(x, y_rows, pos_tok, wgt_tok, g, b, tm, emit_bf16):
    t, d = x.shape
    half = d // 2
    row = lambda i, pos: (i, 0)
    vec = lambda i, pos: (0, 0)
    out_specs = [pl.BlockSpec((tm, d), row)]
    out_shape = [jax.ShapeDtypeStruct((t, d), F32)]
    if emit_bf16:
        out_specs.append(pl.BlockSpec((tm, d), row))
        out_shape.append(jax.ShapeDtypeStruct((t, d), BF16))
    grid_spec = pltpu.PrefetchScalarGridSpec(
        num_scalar_prefetch=1,
        grid=(t // tm,),
        in_specs=[pl.BlockSpec(memory_space=pl.ANY),
                  pl.BlockSpec((tm, d), row), pl.BlockSpec((tm, TOP_K), row),
                  pl.BlockSpec((1, d), vec), pl.BlockSpec((1, d), vec)],
        out_specs=out_specs,
        scratch_shapes=[pltpu.VMEM((2, TOP_K, tm, half), jnp.uint32), pltpu.SemaphoreType.DMA((2,))],
    )
    return pl.pallas_call(
        functools.partial(_combine_body, tm=tm, emit_bf16=emit_bf16),
        grid_spec=grid_spec,
        out_shape=out_shape,
        compiler_params=_params(("arbitrary",)),
        name="moe_combine_layer_norm",
    )(pos_tok, y_rows, x, wgt_tok, g.reshape(1, d), b.reshape(1, d))


def _moe_block(x_f32, x_packed, router_w, router_b, w1, b1, w2, b2, g, b, emit_bf16):
    t = x_f32.shape[0]
    tm = min(MOE_TILE, t)
    idx, wgt = _router(x_f32, router_w, router_b)
    tok_of_row, pos_tok, tile_expert, tile_valid = _routing_tables(idx, tm)
    x_sorted = _dispatch(x_packed, tok_of_row, tm)
    y_rows = _experts(x_sorted, w1, b1, w2, b2, tile_expert, tile_valid, tm)
    return _combine_layer_norm(x_f32, y_rows, pos_tok, wgt.T, g, b, min(COMBINE_TILE, t), emit_bf16)


def _layer0_mixer(x_f32, x_b16, batch, seq, w_in, conv_w, conv_b, dt_bias, a_log, d_skip, norm_w,
                  pool_w, pool_scale, w_out, ln_g, ln_b):
    n_ssd = 2 * D_SSD + 2 * SSD_BC
    h_u = _matmul(x_b16, w_in, BF16, 1024, 512, 0, D_POOL)
    h = _matmul(x_b16, w_in, BF16, 1024, 512, D_POOL, n_ssd)
    w_dt = jnp.pad(w_in[:, D_POOL + n_ssd:], ((0, 0), (0, V7X_LANES - SSD_HEADS)))
    dt_raw = _matmul(x_b16, w_dt, F32, 1024, V7X_LANES)[:, :SSD_HEADS]
    t = batch * seq
    dt_raw = dt_raw.reshape(t, SSD_GROUPS, SSD_GROUP_HEADS).transpose(1, 0, 2)
    dt_raw = jnp.pad(dt_raw, ((0, 0), (0, 0), (0, V7X_LANES - SSD_GROUP_HEADS)))
    y_ssd = _ssd_mixer(h, dt_raw, conv_w, conv_b, dt_bias, a_log, d_skip, norm_w, batch, seq,
                       z_block=0, x_block=D_SSD // SSD_GROUP_DIM,
                       b_block=2 * D_SSD // SSD_STATE, c_block=(2 * D_SSD + SSD_BC) // SSD_STATE)
    y_pool = _pool_mixer(h_u, 0, pool_w.astype(BF16), pool_scale, batch, seq)
    m = _matmul_cat(y_pool, y_ssd, w_out, BF16, 1024, 1024, D_POOL)
    return _residual_layer_norm(x_f32, m, ln_g, ln_b)


def _layer1_mixer(x_f32, x_b16, batch, seq, w_in, conv_w, w_out, ln_g, ln_b):
    h = _matmul(x_b16, w_in, BF16, 1024, 512)
    y_conv = _gated_conv(h, conv_w, batch, seq)
    qb = 3 * D_CONV // SB_HEADDIM
    y_sb = _sb_attention(h, qb, qb + SB_HEADS, qb + 2 * SB_HEADS, batch, seq)
    m = _matmul_cat(y_conv, y_sb, w_out, BF16, 1024, 1024, D_CONV)
    return _residual_layer_norm(x_f32, m, ln_g, ln_b)


def kernel(x, l0_w_in, l0_conv_w, l0_conv_b, l0_dt_bias, l0_a_log, l0_d_skip, l0_ssm_norm_w, l0_pool_w, l0_pool_scale, l0_w_out, l0_ln_mix_g, l0_ln_mix_b, l0_router_w, l0_router_b, l0_w1, l0_b1, l0_w2, l0_b2, l0_ln_ffn_g, l0_ln_ffn_b, l1_w_in, l1_conv_w, l1_w_out, l1_ln_mix_g, l1_ln_mix_b, l1_router_w, l1_router_b, l1_w1, l1_b1, l1_w2, l1_b2, l1_ln_ffn_g, l1_ln_ffn_b):
    batch, seq, d = x.shape
    xf = x.reshape(batch * seq, d)
    xb = xf.astype(BF16)
    xf, xp = _layer0_mixer(xf, xb, batch, seq, l0_w_in, l0_conv_w, l0_conv_b, l0_dt_bias, l0_a_log, l0_d_skip,
                           l0_ssm_norm_w, l0_pool_w, l0_pool_scale, l0_w_out, l0_ln_mix_g, l0_ln_mix_b)
    xf, xb = _moe_block(xf, xp, l0_router_w, l0_router_b, l0_w1, l0_b1, l0_w2, l0_b2, l0_ln_ffn_g, l0_ln_ffn_b,
                        emit_bf16=True)
    xf, xp = _layer1_mixer(xf, xb, batch, seq, l1_w_in, l1_conv_w, l1_w_out, l1_ln_mix_g, l1_ln_mix_b)
    (xf,) = _moe_block(xf, xp, l1_router_w, l1_router_b, l1_w1, l1_b1, l1_w2, l1_b2, l1_ln_ffn_g, l1_ln_ffn_b,
                       emit_bf16=False)
    return xf.reshape(batch, seq, d)
```

```python
import functools

import jax
import jax.numpy as jnp
from jax import lax
from jax.experimental import pallas as pl
from jax.experimental.pallas import tpu as pltpu

F32 = jnp.float32
BF16 = jnp.bfloat16

D_MODEL = 4096
DEPTH = 2
DEEPNORM_ALPHA = (2.0 * DEPTH) ** 0.25
LN_EPS = 1e-5
RMS_EPS = 1e-5

POOL_WINDOWS = (2, 4, 8, 16)
POOL_GROUPS = 4
D_POOL = D_MODEL // 2
POOL_GROUP_DIM = D_POOL // POOL_GROUPS

SSD_HEADDIM = 64
D_SSD = (3 * D_MODEL) // 2
SSD_HEADS = D_SSD // SSD_HEADDIM
SSD_GROUPS = 8
SSD_STATE = 128
SSD_CONV = 4
SSD_GROUP_HEADS = SSD_HEADS // SSD_GROUPS
SSD_GROUP_DIM = D_SSD // SSD_GROUPS
SSD_BC = SSD_GROUPS * SSD_STATE

D_CONV = D_MODEL // 2
CONV_WIDTH = 3
SB_HEADS = 16
SB_HEADDIM = 128
D_SB = SB_HEADS * SB_HEADDIM

N_EXPERTS = 32
TOP_K = 4
D_EXPERT = 512
SWIGLU_LIMIT = 7.0
SWIGLU_ALPHA = 1.702

V7X_LANES = 128
V7X_BF16_SUBLANES = 16
V7X_VMEM_LIMIT = 56 * 1024 * 1024
HALO = V7X_BF16_SUBLANES

SSD_CHUNK = 256
POOL_TILE = 512
CONV_TILE = 512
LN_TILE = 256
ROUTER_TILE = 512
SB_TILE = 256
SB_UNDERFLOW_LOG = -110.0
MOE_TILE = 512
MOE_ISSUE_UNROLL = 8
COMBINE_TILE = 128


def _params(semantics):
    return pltpu.CompilerParams(dimension_semantics=semantics, vmem_limit_bytes=V7X_VMEM_LIMIT)


def _softplus(x):
    return jnp.maximum(x, 0.0) + jnp.log(1.0 + jnp.exp(-jnp.abs(x)))


def _sigmoid(x):
    return 1.0 / (1.0 + jnp.exp(-x))


def _bf16_terms(v, terms):
    out = []
    for _ in range(terms - 1):
        piece = v.astype(BF16)
        out.append(piece)
        v = v - piece.astype(F32)
    out.append(v.astype(BF16))
    return out


def _dot_with_01(v, mat01, terms, v_on_left=True):
    parts = _bf16_terms(v, terms)
    if v_on_left:
        dots = [jnp.dot(p, mat01, preferred_element_type=F32) for p in parts]
    else:
        dots = [jnp.dot(mat01, p, preferred_element_type=F32) for p in parts]
    total = dots[0]
    for d in dots[1:]:
        total = total + d
    return total


def _mm_body(a_ref, w_ref, o_ref, wb_ref):
    @pl.when(pl.program_id(1) == 0)
    def _():
        wb_ref[...] = w_ref[...].astype(BF16)

    o_ref[...] = jnp.dot(a_ref[...], wb_ref[...], preferred_element_type=F32).astype(o_ref.dtype)


def _matmul(a, w, out_dtype, tm, tn, col0=0, ncols=None):
    m, kdim = a.shape
    n = w.shape[1] - col0 if ncols is None else ncols
    tm, tn = min(tm, m), min(tn, n)
    assert m % tm == 0 and n % tn == 0 and col0 % tn == 0 and w.shape[0] == kdim
    cb0 = col0 // tn
    return pl.pallas_call(
        _mm_body,
        grid=(n // tn, m // tm),
        in_specs=[pl.BlockSpec((tm, kdim), lambda j, i: (i, 0)),
                  pl.BlockSpec((kdim, tn), lambda j, i: (0, cb0 + j))],
        out_specs=pl.BlockSpec((tm, tn), lambda j, i: (i, j)),
        out_shape=jax.ShapeDtypeStruct((m, n), out_dtype),
        scratch_shapes=[pltpu.VMEM((kdim, tn), BF16)],
        compiler_params=_params(("parallel", "arbitrary")),
        name="matmul",
    )(a, w)


def _mm_cat_body(a1_ref, a2_ref, w_ref, o_ref, acc_ref, *, n1, nk):
    k = pl.program_id(2)

    def accumulate(a_ref):
        part = jnp.dot(a_ref[...], w_ref[...].astype(BF16), preferred_element_type=F32)

        @pl.when(k == 0)
        def _():
            acc_ref[...] = part

        @pl.when(k > 0)
        def _():
            acc_ref[...] += part

    @pl.when(k < n1)
    def _():
        accumulate(a1_ref)

    @pl.when(k >= n1)
    def _():
        accumulate(a2_ref)

    @pl.when(k == nk - 1)
    def _():
        o_ref[...] = acc_ref[...].astype(o_ref.dtype)


def _matmul_cat(a1, a2, w, out_dtype, tm, tn, tk):
    m, k1 = a1.shape
    k2 = a2.shape[1]
    n = w.shape[1]
    assert k1 % tk == 0 and k2 % tk == 0 and m % tm == 0 and n % tn == 0 and w.shape[0] == k1 + k2
    n1, nk = k1 // tk, (k1 + k2) // tk
    return pl.pallas_call(
        functools.partial(_mm_cat_body, n1=n1, nk=nk),
        grid=(n // tn, m // tm, nk),
        in_specs=[pl.BlockSpec((tm, tk), lambda j, i, k: (i, jnp.minimum(k, n1 - 1))),
                  pl.BlockSpec((tm, tk), lambda j, i, k: (i, jnp.maximum(k - n1, 0))),
                  pl.BlockSpec((tk, tn), lambda j, i, k: (k, j))],
        out_specs=pl.BlockSpec((tm, tn), lambda j, i, k: (i, j)),
        out_shape=jax.ShapeDtypeStruct((m, n), out_dtype),
        scratch_shapes=[pltpu.VMEM((tm, tn), F32)],
        compiler_params=_params(("parallel", "parallel", "arbitrary")),
        name="matmul_cat",
    )(a1, a2, w)


def _layer_norm_rows(v, g, b):
    mu = jnp.mean(v, axis=-1, keepdims=True)
    vc = v - mu
    var = jnp.mean(vc * vc, axis=-1, keepdims=True)
    return vc * lax.rsqrt(var + LN_EPS) * g + b


def _bf16_bits(v):
    return lax.bitcast_convert_type(v.astype(BF16).astype(F32), jnp.uint32)


def _pack_halves(v):
    n = v.shape[1] // 2
    return _bf16_bits(v[:, n:]) | (_bf16_bits(v[:, :n]) >> 16)


def _unpack_halves(words):
    low = lax.bitcast_convert_type(words << 16, F32)
    high = lax.bitcast_convert_type(words & jnp.uint32(0xFFFF0000), F32)
    return low, high


def _ln_body(x_ref, m_ref, g_ref, b_ref, of_ref, op_ref):
    v = DEEPNORM_ALPHA * x_ref[...] + m_ref[...].astype(F32)
    y = _layer_norm_rows(v, g_ref[...], b_ref[...])
    of_ref[...] = y
    op_ref[...] = _pack_halves(y)


def _residual_layer_norm(x, m, g, b):
    t, d = x.shape
    tm = min(LN_TILE, t)
    row = pl.BlockSpec((tm, d), lambda i: (i, 0))
    half = pl.BlockSpec((tm, d // 2), lambda i: (i, 0))
    vec = pl.BlockSpec((1, d), lambda i: (0, 0))
    return pl.pallas_call(
        _ln_body,
        grid=(t // tm,),
        in_specs=[row, row, vec, vec],
        out_specs=[row, half],
        out_shape=[jax.ShapeDtypeStruct((t, d), F32), jax.ShapeDtypeStruct((t, d // 2), jnp.uint32)],
        compiler_params=_params(("parallel",)),
        name="residual_layer_norm",
    )(x, m, g.reshape(1, d), b.reshape(1, d))


def _pool_body(u_ref, halo_ref, w_ref, scale_ref, o_ref, ext_ref, *, ts):
    i = pl.program_id(1)
    keep = jnp.where(i > 0, 1.0, 0.0).astype(F32)
    pos = (i * ts + lax.broadcasted_iota(jnp.int32, (ts, 1), 0) + 1).astype(F32)
    for g, win in enumerate(POOL_WINDOWS):
        cols = slice(g * POOL_GROUP_DIM, (g + 1) * POOL_GROUP_DIM)
        cur = u_ref[:, cols].astype(F32)
        ext_ref[0:HALO, :] = halo_ref[:, cols].astype(F32) * keep
        ext_ref[HALO:HALO + ts, :] = cur
        acc = cur
        for back in range(1, win):
            acc = acc + ext_ref[HALO - back:HALO - back + ts, :]
        mean = acc / jnp.minimum(pos, float(win))
        mixed = jnp.dot((mean - cur).astype(BF16), w_ref[g], preferred_element_type=F32)
        o_ref[:, cols] = (mixed * scale_ref[:, cols]).astype(o_ref.dtype)


def _pool_mixer(h, col_block, pool_w, pool_scale, batch, seq):
    ts = min(POOL_TILE, seq)
    nt = seq // ts
    hb = ts // HALO

    def cur_map(b, i):
        return (b * nt + i, col_block)

    def halo_map(b, i):
        return (jnp.maximum((b * nt + i) * hb - 1, 0), col_block)

    return pl.pallas_call(
        functools.partial(_pool_body, ts=ts),
        grid=(batch, nt),
        in_specs=[pl.BlockSpec((ts, D_POOL), cur_map),
                  pl.BlockSpec((HALO, D_POOL), halo_map),
                  pl.BlockSpec((POOL_GROUPS, POOL_GROUP_DIM, POOL_GROUP_DIM), lambda b, i: (0, 0, 0)),
                  pl.BlockSpec((1, D_POOL), lambda b, i: (0, 0))],
        out_specs=pl.BlockSpec((ts, D_POOL), lambda b, i: (b * nt + i, 0)),
        out_shape=jax.ShapeDtypeStruct((batch * seq, D_POOL), BF16),
        scratch_shapes=[pltpu.VMEM((HALO + ts, POOL_GROUP_DIM), F32)],
        compiler_params=_params(("parallel", "parallel")),
        name="pool_mixer",
    )(h, h, pool_w, pool_scale.reshape(1, D_POOL))


def _conv_silu(cur_ref, halo_ref, w_ref, b_ref, ext_ref, keep, width, ln):
    ext_ref[0:HALO, 0:width] = halo_ref[...].astype(F32) * keep
    ext_ref[HALO:HALO + ln, 0:width] = cur_ref[...].astype(F32)
    acc = b_ref[...]
    for tap in range(SSD_CONV):
        off = HALO - (SSD_CONV - 1) + tap
        acc = acc + w_ref[tap:tap + 1, :] * ext_ref[off:off + ln, 0:width]
    return acc * _sigmoid(acc)


def _ssd_body(z_ref, x_ref, xh_ref, b_ref, bh_ref, c_ref, ch_ref, dt_ref,
              cwx_ref, cbx_ref, cwb_ref, cbb_ref, cwc_ref, cbc_ref,
              dtb_ref, alog_ref, dskip_ref, nw_ref, o_ref, ext_ref, state_ref, *, ln):
    c = pl.program_id(2)
    keep = jnp.where(c > 0, 1.0, 0.0).astype(F32)

    @pl.when(c == 0)
    def _():
        state_ref[...] = jnp.zeros_like(state_ref)

    gh, p, gd = SSD_GROUP_HEADS, SSD_HEADDIM, SSD_GROUP_DIM

    xs = _conv_silu(x_ref, xh_ref, cwx_ref, cbx_ref, ext_ref, keep, gd, ln)
    bm = _conv_silu(b_ref, bh_ref, cwb_ref, cbb_ref, ext_ref, keep, SSD_STATE, ln)
    cm = _conv_silu(c_ref, ch_ref, cwc_ref, cbc_ref, ext_ref, keep, SSD_STATE, ln)

    dt = _softplus(dt_ref[...] + dtb_ref[...])
    da = dt * (-jnp.exp(alog_ref[...]))

    row = lax.broadcasted_iota(jnp.int32, (ln, ln), 0)
    col = lax.broadcasted_iota(jnp.int32, (ln, ln), 1)
    causal = row >= col
    tri = jnp.where(causal, 1.0, 0.0).astype(BF16)
    acum = _dot_with_01(da, tri, 3, v_on_left=False)
    acum_t = acum.T

    eh = lax.broadcasted_iota(jnp.int32, (V7X_LANES, gd), 0)
    ec = lax.broadcasted_iota(jnp.int32, (V7X_LANES, gd), 1)
    expand = jnp.where((ec >= eh * p) & (ec < (eh + 1) * p), 1.0, 0.0).astype(BF16)
    dt_x = _dot_with_01(dt, expand, 2)
    acum_x = _dot_with_01(acum, expand, 3)
    last_x = acum_x[ln - 1:ln, :]

    xdt = xs * dt_x
    xdt_b = xdt.astype(BF16)
    bm_b = bm.astype(BF16)
    cm_b = cm.astype(BF16)

    cb = lax.dot_general(cm_b, bm_b, (((1,), (1,)), ((), ())), preferred_element_type=F32)

    def head_diag(e, x_pair):
        seg = acum[:, e:e + 1] - acum_t[e:e + 1, :]
        decay = jnp.where(causal, jnp.exp(jnp.minimum(seg, 0.0)), 0.0)
        return jnp.dot((cb * decay).astype(BF16), x_pair, preferred_element_type=F32)

    first_half = lax.broadcasted_iota(jnp.int32, (ln, 2 * p), 1) < p
    parts = []
    for pair in range(gh // 2):
        x_pair = xdt_b[:, pair * 2 * p:(pair + 1) * 2 * p]
        parts.append(jnp.where(first_half, head_diag(2 * pair, x_pair), head_diag(2 * pair + 1, x_pair)))
    y_diag = jnp.concatenate(parts, axis=1)

    state = state_ref[...]
    y_off = jnp.dot(cm_b, state.astype(BF16), preferred_element_type=F32) * jnp.exp(acum_x)
    to_end = jnp.exp(last_x - acum_x)
    upd = jnp.dot(bm.T.astype(BF16), (xdt * to_end).astype(BF16), preferred_element_type=F32)
    state_ref[...] = state * jnp.exp(last_x) + upd

    y = y_diag + y_off + dskip_ref[...] * xs
    zf = z_ref[...].astype(F32)
    y = y * (zf * _sigmoid(zf))
    y = y * lax.rsqrt(jnp.mean(y * y, axis=-1, keepdims=True) + RMS_EPS)
    o_ref[...] = (y * nw_ref[...]).astype(o_ref.dtype)


def _ssd_mixer(h, dt_raw, conv_w, conv_b, dt_bias, a_log, d_skip, norm_w, batch, seq,
               z_block, x_block, b_block, c_block):
    ln = min(SSD_CHUNK, seq)
    nc = seq // ln
    hb = ln // HALO
    gd, gh = SSD_GROUP_DIM, SSD_GROUP_HEADS
    ngb = D_SSD // SSD_STATE

    def cur(base):
        return lambda b, g, c: (b * nc + c, base + g)

    def halo(base):
        return lambda b, g, c: (jnp.maximum((b * nc + c) * hb - 1, 0), base + g)

    grp = lambda b, g, c: (g, 0, 0)
    in_specs = [
        pl.BlockSpec((ln, gd), cur(z_block)),
        pl.BlockSpec((ln, gd), cur(x_block)), pl.BlockSpec((HALO, gd), halo(x_block)),
        pl.BlockSpec((ln, SSD_STATE), cur(b_block)), pl.BlockSpec((HALO, SSD_STATE), halo(b_block)),
        pl.BlockSpec((ln, SSD_STATE), cur(c_block)), pl.BlockSpec((HALO, SSD_STATE), halo(c_block)),
        pl.BlockSpec((None, ln, V7X_LANES), lambda b, g, c: (g, b * nc + c, 0)),
        pl.BlockSpec((SSD_CONV, gd), lambda b, g, c: (0, g)), pl.BlockSpec((1, gd), lambda b, g, c: (0, g)),
        pl.BlockSpec((SSD_CONV, SSD_STATE), lambda b, g, c: (0, ngb + g)),
        pl.BlockSpec((1, SSD_STATE), lambda b, g, c: (0, ngb + g)),
        pl.BlockSpec((SSD_CONV, SSD_STATE), lambda b, g, c: (0, ngb + SSD_GROUPS + g)),
        pl.BlockSpec((1, SSD_STATE), lambda b, g, c: (0, ngb + SSD_GROUPS + g)),
        pl.BlockSpec((None, 1, V7X_LANES), grp), pl.BlockSpec((None, 1, V7X_LANES), grp),
        pl.BlockSpec((1, gd), lambda b, g, c: (0, g)), pl.BlockSpec((1, gd), lambda b, g, c: (0, g)),
    ]
    conv_b2 = conv_b.reshape(1, -1)
    d_skip_x = jnp.repeat(d_skip.astype(F32), SSD_HEADDIM).reshape(1, D_SSD)

    def per_head(v):
        return jnp.pad(v.astype(F32).reshape(SSD_GROUPS, 1, gh), ((0, 0), (0, 0), (0, V7X_LANES - gh)))
    return pl.pallas_call(
        functools.partial(_ssd_body, ln=ln),
        grid=(batch, SSD_GROUPS, nc),
        in_specs=in_specs,
        out_specs=pl.BlockSpec((ln, gd), lambda b, g, c: (b * nc + c, g)),
        out_shape=jax.ShapeDtypeStruct((batch * seq, D_SSD), BF16),
        scratch_shapes=[pltpu.VMEM((HALO + ln, gd), F32), pltpu.VMEM((SSD_STATE, gd), F32)],
        compiler_params=_params(("parallel", "parallel", "arbitrary")),
        name="ssd_mixer",
    )(h, h, h, h, h, h, h, dt_raw,
      conv_w, conv_b2, conv_w, conv_b2, conv_w, conv_b2,
      per_head(dt_bias), per_head(a_log), d_skip_x, norm_w.reshape(1, D_SSD))


def _gconv_body(bg_ref, cg_ref, cgh_ref, xi_ref, xih_ref, w_ref, o_ref, ext_ref, *, ts):
    i = pl.program_id(1)
    keep = jnp.where(i > 0, 1.0, 0.0).astype(F32)
    ext_ref[0:HALO, :] = cgh_ref[...].astype(F32) * xih_ref[...].astype(F32) * keep
    ext_ref[HALO:HALO + ts, :] = cg_ref[...].astype(F32) * xi_ref[...].astype(F32)
    acc = jnp.zeros((ts, D_CONV), F32)
    for tap in range(CONV_WIDTH):
        off = HALO - (CONV_WIDTH - 1) + tap
        acc = acc + w_ref[tap:tap + 1, :] * ext_ref[off:off + ts, :]
    o_ref[...] = (bg_ref[...].astype(F32) * acc).astype(o_ref.dtype)


def _gated_conv(h, conv_w, batch, seq):
    ts = min(CONV_TILE, seq)
    nt = seq // ts
    hb = ts // HALO

    def cur(blk):
        return lambda b, i: (b * nt + i, blk)

    def halo(blk):
        return lambda b, i: (jnp.maximum((b * nt + i) * hb - 1, 0), blk)

    return pl.pallas_call(
        functools.partial(_gconv_body, ts=ts),
        grid=(batch, nt),
        in_specs=[pl.BlockSpec((ts, D_CONV), cur(0)),
                  pl.BlockSpec((ts, D_CONV), cur(1)), pl.BlockSpec((HALO, D_CONV), halo(1)),
                  pl.BlockSpec((ts, D_CONV), cur(2)), pl.BlockSpec((HALO, D_CONV), halo(2)),
                  pl.BlockSpec((CONV_WIDTH, D_CONV), lambda b, i: (0, 0))],
        out_specs=pl.BlockSpec((ts, D_CONV), lambda b, i: (b * nt + i, 0)),
        out_shape=jax.ShapeDtypeStruct((batch * seq, D_CONV), BF16),
        scratch_shapes=[pltpu.VMEM((HALO + ts, D_CONV), F32)],
        compiler_params=_params(("parallel", "parallel")),
        name="gated_conv",
    )(h, h, h, h, h, conv_w)


def _sb_block(q, k_ref, v_ref, start, tk, carry, acc, upper, mask):
    kt = k_ref[pl.ds(start, tk), :]
    vt = v_ref[pl.ds(start, tk), :]
    z = lax.dot_general(q, kt, (((1,), (1,)), ((), ())), preferred_element_type=F32) * (SB_HEADDIM ** -0.5)
    sp = _softplus(z)
    log_stay = -sp if mask is None else jnp.where(mask, -sp, 0.0)
    hi = log_stay.astype(BF16)
    lo = (log_stay - hi.astype(F32)).astype(BF16)
    after = (jnp.dot(hi, upper, preferred_element_type=F32)
             + jnp.dot(lo, upper, preferred_element_type=F32)) + carry
    w = jnp.exp((z - sp) + after)
    if mask is not None:
        w = jnp.where(mask, w, 0.0)
    acc = acc + jnp.dot(w.astype(BF16), vt, preferred_element_type=F32)
    carry = carry + jnp.sum(log_stay, axis=1, keepdims=True)
    return carry, acc


def _sb_body(q_ref, k_ref, v_ref, o_ref, *, tq):
    qi = pl.program_id(2)
    q = q_ref[...]
    row = lax.broadcasted_iota(jnp.int32, (tq, tq), 0)
    col = lax.broadcasted_iota(jnp.int32, (tq, tq), 1)
    upper = jnp.where(row > col, 1.0, 0.0).astype(BF16)
    diag_mask = col < row

    def tile(j, carry, acc, mask):
        return _sb_block(q, k_ref, v_ref, pl.multiple_of(j * tq, tq), tq, carry, acc, upper, mask)

    def tile_pair(j, carry, acc):
        carry, acc = tile(j, carry, acc, None)
        has_second = jnp.broadcast_to(j >= 1, (tq, tq))
        return tile(jnp.maximum(j - 1, 0), carry, acc, has_second)

    carry = jnp.zeros((tq, 1), F32)
    acc = jnp.zeros((tq, SB_HEADDIM), F32)
    carry, acc = tile(qi, carry, acc, diag_mask)
    has_prev = jnp.broadcast_to(qi >= 1, (tq, tq))
    carry, acc = tile(jnp.maximum(qi - 1, 0), carry, acc, has_prev)

    def more(state):
        j, carry, _ = state
        return jnp.logical_and(j >= 0, jnp.max(carry) > SB_UNDERFLOW_LOG)

    def step(state):
        j, carry, acc = state
        carry, acc = tile_pair(j, carry, acc)
        return j - 2, carry, acc

    _, _, acc = lax.while_loop(more, step, (qi - 2, carry, acc))
    o_ref[...] = acc.astype(o_ref.dtype)


def _sb_attention(h, q_block, k_block, v_block, batch, seq):
    tq = min(SB_TILE, seq)
    nq = seq // tq
    return pl.pallas_call(
        functools.partial(_sb_body, tq=tq),
        grid=(batch, SB_HEADS, nq),
        in_specs=[pl.BlockSpec((tq, SB_HEADDIM), lambda b, hd, i: (b * nq + i, q_block + hd)),
                  pl.BlockSpec((seq, SB_HEADDIM), lambda b, hd, i: (b, k_block + hd)),
                  pl.BlockSpec((seq, SB_HEADDIM), lambda b, hd, i: (b, v_block + hd))],
        out_specs=pl.BlockSpec((tq, SB_HEADDIM), lambda b, hd, i: (b * nq + i, hd)),
        out_shape=jax.ShapeDtypeStruct((batch * seq, D_SB), BF16),
        compiler_params=_params(("parallel", "parallel", "arbitrary")),
        name="stick_breaking_attention",
    )(h, h, h)


def _router_body(x_ref, wt_ref, b_ref, idx_ref, wgt_ref):
    w_hi, w_lo = _bf16_terms(wt_ref[...], 2)
    x_hi, x_lo = _bf16_terms(x_ref[...], 2)
    nt_dot = lambda a, b: lax.dot_general(a, b, (((1,), (1,)), ((), ())), preferred_element_type=F32)
    logits = (nt_dot(w_hi, x_hi) + (nt_dot(w_hi, x_lo) + nt_dot(w_lo, x_hi))) + b_ref[...]
    iota = lax.broadcasted_iota(jnp.int32, logits.shape, 0).astype(F32)
    vals, idxs = [], []
    for _ in range(TOP_K):
        mx = jnp.max(logits, axis=0, keepdims=True)
        ix = jnp.min(jnp.where(logits == mx, iota, float(N_EXPERTS)), axis=0, keepdims=True)
        vals.append(mx)
        idxs.append(ix)
        logits = jnp.where(iota == ix, -jnp.inf, logits)
    exps = [jnp.exp(v - vals[0]) for v in vals]
    denom = exps[0] + exps[1] + exps[2] + exps[3]
    for k in range(TOP_K):
        idx_ref[k:k + 1, :] = idxs[k].astype(jnp.int32)
        wgt_ref[k:k + 1, :] = exps[k] / denom


def _router(x, router_w, router_b):
    t, d = x.shape
    tm = min(ROUTER_TILE, t)
    out = pl.BlockSpec((TOP_K, tm), lambda i: (0, i))
    return pl.pallas_call(
        _router_body,
        grid=(t // tm,),
        in_specs=[pl.BlockSpec((tm, d), lambda i: (i, 0)),
                  pl.BlockSpec((N_EXPERTS, d), lambda i: (0, 0)),
                  pl.BlockSpec((N_EXPERTS, 1), lambda i: (0, 0))],
        out_specs=[out, out],
        out_shape=[jax.ShapeDtypeStruct((TOP_K, t), jnp.int32), jax.ShapeDtypeStruct((TOP_K, t), F32)],
        compiler_params=_params(("parallel",)),
        name="moe_router",
    )(x, router_w.T, router_b.reshape(N_EXPERTS, 1))


def _routing_tables(idx, tm):
    t = idx.shape[1]
    pairs = TOP_K * t
    rows = pairs + N_EXPERTS * tm
    ntiles = rows // tm
    e_flat = idx.reshape(pairs)
    onehot = (e_flat[:, None] == jnp.arange(N_EXPERTS, dtype=jnp.int32)[None, :]).astype(jnp.int32)
    csum = jnp.cumsum(onehot, axis=0)
    rank = jnp.sum((csum - onehot) * onehot, axis=1)
    counts = csum[-1]
    padded = ((counts + tm - 1) // tm) * tm
    gend = jnp.cumsum(padded)
    gstart = gend - padded
    pos = (gstart[e_flat] + rank).astype(jnp.int32)
    token = (jnp.arange(pairs, dtype=jnp.int32) % t)
    tok_of_row = jnp.zeros((rows,), jnp.int32).at[pos].set(token)
    tile_start = jnp.arange(ntiles, dtype=jnp.int32) * tm
    tile_expert = jnp.sum((tile_start[:, None] >= gend[None, :]).astype(jnp.int32), axis=1)
    tile_expert = jnp.minimum(tile_expert, N_EXPERTS - 1)
    tile_valid = (tile_start < gend[-1]).astype(jnp.int32)
    pos_tok = pos.reshape(TOP_K, t).T.reshape(pairs)
    return tok_of_row, pos_tok, tile_expert, tile_valid


def _dispatch_body(tok_ref, x_hbm, o_ref, buf, sem, *, batch):
    i = pl.program_id(0)
    nt = pl.num_programs(0)

    def issue(step, slot):
        base = step * batch

        def body(r, carry):
            pltpu.make_async_copy(x_hbm.at[pl.ds(tok_ref[base + r], 1), :], buf.at[slot, pl.ds(r, 1), :],
                                  sem.at[slot]).start()
            return carry

        lax.fori_loop(0, batch, body, 0, unroll=MOE_ISSUE_UNROLL)

    @pl.when(i == 0)
    def _():
        issue(0, 0)

    @pl.when(i + 1 < nt)
    def _():
        issue(i + 1, (i + 1) % 2)

    slot = i % 2
    pltpu.make_async_copy(x_hbm.at[pl.ds(0, batch), :], buf.at[slot], sem.at[slot]).wait()
    o_ref[...] = buf[slot]


def _dispatch(x_packed, tok_of_row, batch):
    half = x_packed.shape[1]
    rows = tok_of_row.shape[0]
    grid_spec = pltpu.PrefetchScalarGridSpec(
        num_scalar_prefetch=1,
        grid=(rows // batch,),
        in_specs=[pl.BlockSpec(memory_space=pl.ANY)],
        out_specs=pl.BlockSpec((batch, half), lambda i, tok: (i, 0)),
        scratch_shapes=[pltpu.VMEM((2, batch, half), jnp.uint32), pltpu.SemaphoreType.DMA((2,))],
    )
    return pl.pallas_call(
        functools.partial(_dispatch_body, batch=batch),
        grid_spec=grid_spec,
        out_shape=jax.ShapeDtypeStruct((rows, half), jnp.uint32),
        compiler_params=_params(("arbitrary",)),
        name="moe_dispatch",
    )(tok_of_row, x_packed)


def _expert_changed(te_ref, i):
    return jnp.logical_or(i == 0, te_ref[i] != te_ref[jnp.maximum(i - 1, 0)])


def _up_body(te_ref, tv_ref, x_ref, w1_ref, b1_ref, o_ref, w1b_ref):
    i = pl.program_id(0)
    half = x_ref.shape[1]

    @pl.when(_expert_changed(te_ref, i))
    def _():
        w1b_ref[...] = w1_ref[...].astype(BF16)

    @pl.when(tv_ref[i] == 1)
    def _():
        low, high = _unpack_halves(x_ref[...])
        hh = (jnp.dot(low.astype(BF16), w1b_ref[0:half, :], preferred_element_type=F32)
              + jnp.dot(high.astype(BF16), w1b_ref[half:2 * half, :], preferred_element_type=F32)) + b1_ref[...]
        glu = jnp.minimum(hh[:, :D_EXPERT], SWIGLU_LIMIT)
        lin = jnp.clip(hh[:, D_EXPERT:], -SWIGLU_LIMIT, SWIGLU_LIMIT)
        o_ref[...] = (glu * _sigmoid(SWIGLU_ALPHA * glu) * (lin + 1.0)).astype(o_ref.dtype)

    @pl.when(tv_ref[i] == 0)
    def _():
        o_ref[...] = jnp.zeros_like(o_ref)


def _down_body(te_ref, tv_ref, a_ref, w2_ref, b2_ref, o_ref, w2b_ref):
    i = pl.program_id(0)

    @pl.when(_expert_changed(te_ref, i))
    def _():
        w2b_ref[...] = w2_ref[...].astype(BF16)

    @pl.when(tv_ref[i] == 1)
    def _():
        out = jnp.dot(a_ref[...], w2b_ref[...], preferred_element_type=F32) + b2_ref[...]
        o_ref[...] = _pack_halves(out)

    @pl.when(tv_ref[i] == 0)
    def _():
        o_ref[...] = jnp.zeros_like(o_ref)


def _experts(x_sorted, w1, b1, w2, b2, tile_expert, tile_valid, tm):
    rows, half = x_sorted.shape
    d = 2 * half
    ntiles = rows // tm
    by_expert = lambda i, te, tv: (te[i], 0, 0)
    by_tile = lambda i, te, tv: (i, 0)
    act = pl.pallas_call(
        _up_body,
        grid_spec=pltpu.PrefetchScalarGridSpec(
            num_scalar_prefetch=2,
            grid=(ntiles,),
            in_specs=[pl.BlockSpec((tm, half), by_tile),
                      pl.BlockSpec((None, d, 2 * D_EXPERT), by_expert),
                      pl.BlockSpec((None, 1, 2 * D_EXPERT), by_expert)],
            out_specs=pl.BlockSpec((tm, D_EXPERT), by_tile),
            scratch_shapes=[pltpu.VMEM((d, 2 * D_EXPERT), BF16)],
        ),
        out_shape=jax.ShapeDtypeStruct((rows, D_EXPERT), BF16),
        compiler_params=_params(("arbitrary",)),
        name="moe_up",
    )(tile_expert, tile_valid, x_sorted, w1, b1.reshape(N_EXPERTS, 1, -1))
    return pl.pallas_call(
        _down_body,
        grid_spec=pltpu.PrefetchScalarGridSpec(
            num_scalar_prefetch=2,
            grid=(ntiles,),
            in_specs=[pl.BlockSpec((tm, D_EXPERT), by_tile),
                      pl.BlockSpec((None, D_EXPERT, d), by_expert),
                      pl.BlockSpec((None, 1, d), by_expert)],
            out_specs=pl.BlockSpec((tm, half), by_tile),
            scratch_shapes=[pltpu.VMEM((D_EXPERT, d), BF16)],
        ),
        out_shape=jax.ShapeDtypeStruct((rows, half), jnp.uint32),
        compiler_params=_params(("arbitrary",)),
        name="moe_down",
    )(tile_expert, tile_valid, act, w2, b2.reshape(N_EXPERTS, 1, -1))


def _combine_body(pos_ref, y_hbm, x_ref, wgt_ref, g_ref, b_ref, of_ref, *rest, tm, emit_bf16):
    ob_ref = rest[0] if emit_bf16 else None
    ybuf, sem = rest[-2:]
    i = pl.program_id(0)
    nt = pl.num_programs(0)

    def row_copy(src, slot, k, r):
        return pltpu.make_async_copy(y_hbm.at[pl.ds(src, 1), :], ybuf.at[slot, k, pl.ds(r, 1), :], sem.at[slot])

    def issue(tile, slot):
        base = tile * tm * TOP_K

        def body(r, carry):
            for k in range(TOP_K):
                row_copy(pos_ref[base + r * TOP_K + k], slot, k, r).start()
            return carry

        lax.fori_loop(0, tm, body, 0, unroll=4)

    @pl.when(i == 0)
    def _():
        issue(0, 0)

    @pl.when(i + 1 < nt)
    def _():
        issue(i + 1, (i + 1) % 2)

    slot = i % 2
    for k in range(TOP_K):
        pltpu.make_async_copy(y_hbm.at[pl.ds(0, tm), :], ybuf.at[slot, k], sem.at[slot]).wait()
    wgt = wgt_ref[...]
    moe_low, moe_high = None, None
    for k in range(TOP_K):
        low, high = _unpack_halves(ybuf[slot, k])
        gk = wgt[:, k:k + 1]
        moe_low = gk * low if k == 0 else moe_low + gk * low
        moe_high = gk * high if k == 0 else moe_high + gk * high
    moe = jnp.concatenate([moe_low, moe_high], axis=1)
    y = _layer_norm_rows(DEEPNORM_ALPHA * x_ref[...] + moe, g_ref[...], b_ref[...])
    of_ref[...] = y
    if emit_bf16:
        ob_ref[...] = y.astype(BF16)


def _combine_layer_norm(x, y_rows, pos_tok, wgt_tok, g, b, tm, emit_bf16):
    t, d = x.shape
    half = d // 2
    row = lambda i, pos: (i, 0)
    vec = lambda i, pos: (0, 0)
    out_specs = [pl.BlockSpec((tm, d), row)]
    out_shape = [jax.ShapeDtypeStruct((t, d), F32)]
    if emit_bf16:
        out_specs.append(pl.BlockSpec((tm, d), row))
        out_shape.append(jax.ShapeDtypeStruct((t, d), BF16))
    grid_spec = pltpu.PrefetchScalarGridSpec(
        num_scalar_prefetch=1,
        grid=(t // tm,),
        in_specs=[pl.BlockSpec(memory_space=pl.ANY),
                  pl.BlockSpec((tm, d), row), pl.BlockSpec((tm, TOP_K), row),
                  pl.BlockSpec((1, d), vec), pl.BlockSpec((1, d), vec)],
        out_specs=out_specs,
        scratch_shapes=[pltpu.VMEM((2, TOP_K, tm, half), jnp.uint32), pltpu.SemaphoreType.DMA((2,))],
    )
    return pl.pallas_call(
        functools.partial(_combine_body, tm=tm, emit_bf16=emit_bf16),
        grid_spec=grid_spec,
        out_shape=out_shape,
        compiler_params=_params(("arbitrary",)),
        name="moe_combine_layer_norm",
    )(pos_tok, y_rows, x, wgt_tok, g.reshape(1, d), b.reshape(1, d))


def _moe_block(x_f32, x_packed, router_w, router_b, w1, b1, w2, b2, g, b, emit_bf16):
    t = x_f32.shape[0]
    tm = min(MOE_TILE, t)
    idx, wgt = _router(x_f32, router_w, router_b)
    tok_of_row, pos_tok, tile_expert, tile_valid = _routing_tables(idx, tm)
    x_sorted = _dispatch(x_packed, tok_of_row, tm)
    y_rows = _experts(x_sorted, w1, b1, w2, b2, tile_expert, tile_valid, tm)
    return _combine_layer_norm(x_f32, y_rows, pos_tok, wgt.T, g, b, min(COMBINE_TILE, t), emit_bf16)


def _layer0_mixer(x_f32, x_b16, batch, seq, w_in, conv_w, conv_b, dt_bias, a_log, d_skip, norm_w,
                  pool_w, pool_scale, w_out, ln_g, ln_b):
    n_ssd = 2 * D_SSD + 2 * SSD_BC
    h_u = _matmul(x_b16, w_in, BF16, 1024, 512, 0, D_POOL)
    h = _matmul(x_b16, w_in, BF16, 1024, 512, D_POOL, n_ssd)
    w_dt = jnp.pad(w_in[:, D_POOL + n_ssd:], ((0, 0), (0, V7X_LANES - SSD_HEADS)))
    dt_raw = _matmul(x_b16, w_dt, F32, 1024, V7X_LANES)[:, :SSD_HEADS]
    t = batch * seq
    dt_raw = dt_raw.reshape(t, SSD_GROUPS, SSD_GROUP_HEADS).transpose(1, 0, 2)
    dt_raw = jnp.pad(dt_raw, ((0, 0), (0, 0), (0, V7X_LANES - SSD_GROUP_HEADS)))
    y_ssd = _ssd_mixer(h, dt_raw, conv_w, conv_b, dt_bias, a_log, d_skip, norm_w, batch, seq,
                       z_block=0, x_block=D_SSD // SSD_GROUP_DIM,
                       b_block=2 * D_SSD // SSD_STATE, c_block=(2 * D_SSD + SSD_BC) // SSD_STATE)
    y_pool = _pool_mixer(h_u, 0, pool_w.astype(BF16), pool_scale, batch, seq)
    m = _matmul_cat(y_pool, y_ssd, w_out, BF16, 1024, 1024, D_POOL)
    return _residual_layer_norm(x_f32, m, ln_g, ln_b)


def _layer1_mixer(x_f32, x_b16, batch, seq, w_in, conv_w, w_out, ln_g, ln_b):
    h = _matmul(x_b16, w_in, BF16, 1024, 512)
    y_conv = _gated_conv(h, conv_w, batch, seq)
    qb = 3 * D_CONV // SB_HEADDIM
    y_sb = _sb_attention(h, qb, qb + SB_HEADS, qb + 2 * SB_HEADS, batch, seq)
    m = _matmul_cat(y_conv, y_sb, w_out, BF16, 1024, 1024, D_CONV)
    return _residual_layer_norm(x_f32, m, ln_g, ln_b)


def kernel(x, l0_w_in, l0_conv_w, l0_conv_b, l0_dt_bias, l0_a_log, l0_d_skip, l0_ssm_norm_w, l0_pool_w, l0_pool_scale, l0_w_out, l0_ln_mix_g, l0_ln_mix_b, l0_router_w, l0_router_b, l0_w1, l0_b1, l0_w2, l0_b2, l0_ln_ffn_g, l0_ln_ffn_b, l1_w_in, l1_conv_w, l1_w_out, l1_ln_mix_g, l1_ln_mix_b, l1_router_w, l1_router_b, l1_w1, l1_b1, l1_w2, l1_b2, l1_ln_ffn_g, l1_ln_ffn_b):
    batch, seq, d = x.shape
    xf = x.reshape(batch * seq, d)
    xb = xf.astype(BF16)
    xf, xp = _layer0_mixer(xf, xb, batch, seq, l0_w_in, l0_conv_w, l0_conv_b, l0_dt_bias, l0_a_log, l0_d_skip,
                           l0_ssm_norm_w, l0_pool_w, l0_pool_scale, l0_w_out, l0_ln_mix_g, l0_ln_mix_b)
    xf, xb = _moe_block(xf, xp, l0_router_w, l0_router_b, l0_w1, l0_b1, l0_w2, l0_b2, l0_ln_ffn_g, l0_ln_ffn_b,
                        emit_bf16=True)
    xf, xp = _layer1_mixer(xf, xb, batch, seq, l1_w_in, l1_conv_w, l1_w_out, l1_ln_mix_g, l1_ln_mix_b)
    (xf,) = _moe_block(xf, xp, l1_router_w, l1_router_b, l1_w1, l1_b1, l1_w2, l1_b2, l1_ln_ffn_g, l1_ln_ffn_b,
                       emit_bf16=False)
    return xf.reshape(batch, seq, d)
```

```python
import functools

import jax
import jax.numpy as jnp
from jax import lax
from jax.experimental import pallas as pl
from jax.experimental.pallas import tpu as pltpu

F32 = jnp.float32
BF16 = jnp.bfloat16

D_MODEL = 4096
DEPTH = 2
DEEPNORM_ALPHA = (2.0 * DEPTH) ** 0.25
LN_EPS = 1e-5
RMS_EPS = 1e-5

POOL_WINDOWS = (2, 4, 8, 16)
POOL_GROUPS = 4
D_POOL = D_MODEL // 2
POOL_GROUP_DIM = D_POOL // POOL_GROUPS

SSD_HEADDIM = 64
D_SSD = (3 * D_MODEL) // 2
SSD_HEADS = D_SSD // SSD_HEADDIM
SSD_GROUPS = 8
SSD_STATE = 128
SSD_CONV = 4
SSD_GROUP_HEADS = SSD_HEADS // SSD_GROUPS
SSD_GROUP_DIM = D_SSD // SSD_GROUPS
SSD_BC = SSD_GROUPS * SSD_STATE

D_CONV = D_MODEL // 2
CONV_WIDTH = 3
SB_HEADS = 16
SB_HEADDIM = 128
D_SB = SB_HEADS * SB_HEADDIM

N_EXPERTS = 32
TOP_K = 4
D_EXPERT = 512
SWIGLU_LIMIT = 7.0
SWIGLU_ALPHA = 1.702

V7X_LANES = 128
V7X_BF16_SUBLANES = 16
V7X_VMEM_LIMIT = 56 * 1024 * 1024
HALO = V7X_BF16_SUBLANES

SSD_CHUNK = 256
POOL_TILE = 512
CONV_TILE = 512
LN_TILE = 256
ROUTER_TILE = 512
SB_TILE = 256
SB_UNDERFLOW_LOG = -110.0
MOE_TILE = 512
MOE_ISSUE_UNROLL = 8
COMBINE_TILE = 128


def _params(semantics):
    return pltpu.CompilerParams(dimension_semantics=semantics, vmem_limit_bytes=V7X_VMEM_LIMIT)


def _softplus(x):
    return jnp.maximum(x, 0.0) + jnp.log(1.0 + jnp.exp(-jnp.abs(x)))


def _sigmoid(x):
    return 1.0 / (1.0 + jnp.exp(-x))


def _bf16_terms(v, terms):
    out = []
    for _ in range(terms - 1):
        piece = v.astype(BF16)
        out.append(piece)
        v = v - piece.astype(F32)
    out.append(v.astype(BF16))
    return out


def _dot_with_01(v, mat01, terms, v_on_left=True):
    parts = _bf16_terms(v, terms)
    if v_on_left:
        dots = [jnp.dot(p, mat01, preferred_element_type=F32) for p in parts]
    else:
        dots = [jnp.dot(mat01, p, preferred_element_type=F32) for p in parts]
    total = dots[0]
    for d in dots[1:]:
        total = total + d
    return total


def _mm_body(a_ref, w_ref, o_ref, wb_ref):
    @pl.when(pl.program_id(1) == 0)
    def _():
        wb_ref[...] = w_ref[...].astype(BF16)

    o_ref[...] = jnp.dot(a_ref[...], wb_ref[...], preferred_element_type=F32).astype(o_ref.dtype)


def _matmul(a, w, out_dtype, tm, tn, col0=0, ncols=None):
    m, kdim = a.shape
    n = w.shape[1] - col0 if ncols is None else ncols
    tm, tn = min(tm, m), min(tn, n)
    assert m % tm == 0 and n % tn == 0 and col0 % tn == 0 and w.shape[0] == kdim
    cb0 = col0 // tn
    return pl.pallas_call(
        _mm_body,
        grid=(n // tn, m // tm),
        in_specs=[pl.BlockSpec((tm, kdim), lambda j, i: (i, 0)),
                  pl.BlockSpec((kdim, tn), lambda j, i: (0, cb0 + j))],
        out_specs=pl.BlockSpec((tm, tn), lambda j, i: (i, j)),
        out_shape=jax.ShapeDtypeStruct((m, n), out_dtype),
        scratch_shapes=[pltpu.VMEM((kdim, tn), BF16)],
        compiler_params=_params(("parallel", "arbitrary")),
        name="matmul",
    )(a, w)


def _mm_cat_body(a1_ref, a2_ref, w_ref, o_ref, acc_ref, *, n1, nk):
    k = pl.program_id(2)

    def accumulate(a_ref):
        part = jnp.dot(a_ref[...], w_ref[...].astype(BF16), preferred_element_type=F32)

        @pl.when(k == 0)
        def _():
            acc_ref[...] = part

        @pl.when(k > 0)
        def _():
            acc_ref[...] += part

    @pl.when(k < n1)
    def _():
        accumulate(a1_ref)

    @pl.when(k >= n1)
    def _():
        accumulate(a2_ref)

    @pl.when(k == nk - 1)
    def _():
        o_ref[...] = acc_ref[...].astype(o_ref.dtype)


def _matmul_cat(a1, a2, w, out_dtype, tm, tn, tk):
    m, k1 = a1.shape
    k2 = a2.shape[1]
    n = w.shape[1]
    assert k1 % tk == 0 and k2 % tk == 0 and m % tm == 0 and n % tn == 0 and w.shape[0] == k1 + k2
    n1, nk = k1 // tk, (k1 + k2) // tk
    return pl.pallas_call(
        functools.partial(_mm_cat_body, n1=n1, nk=nk),
        grid=(n // tn, m // tm, nk),
        in_specs=[pl.BlockSpec((tm, tk), lambda j, i, k: (i, jnp.minimum(k, n1 - 1))),
                  pl.BlockSpec((tm, tk), lambda j, i, k: (i, jnp.maximum(k - n1, 0))),
                  pl.BlockSpec((tk, tn), lambda j, i, k: (k, j))],
        out_specs=pl.BlockSpec((tm, tn), lambda j, i, k: (i, j)),
        out_shape=jax.ShapeDtypeStruct((m, n), out_dtype),
        scratch_shapes=[pltpu.VMEM((tm, tn), F32)],
        compiler_params=_params(("parallel", "parallel", "arbitrary")),
        name="matmul_cat",
    )(a1, a2, w)


def _layer_norm_rows(v, g, b):
    mu = jnp.mean(v, axis=-1, keepdims=True)
    vc = v - mu
    var = jnp.mean(vc * vc, axis=-1, keepdims=True)
    return vc * lax.rsqrt(var + LN_EPS) * g + b


def _bf16_bits(v):
    return lax.bitcast_convert_type(v.astype(BF16).astype(F32), jnp.uint32)


def _pack_halves(v):
    n = v.shape[1] // 2
    return _bf16_bits(v[:, n:]) | (_bf16_bits(v[:, :n]) >> 16)


def _unpack_halves(words):
    low = lax.bitcast_convert_type(words << 16, F32)
    high = lax.bitcast_convert_type(words & jnp.uint32(0xFFFF0000), F32)
    return low, high


ROW_SUBS = (D_MODEL // 2) // V7X_LANES


def _store_token_rows(ref, words):
    n = words.shape[0]
    for s in range(ROW_SUBS):
        ref[pl.ds(s, n, stride=ROW_SUBS), :] = words[:, s * V7X_LANES:(s + 1) * V7X_LANES]


def _load_token_rows(ref, n):
    return jnp.concatenate([ref[pl.ds(s, n, stride=ROW_SUBS), :] for s in range(ROW_SUBS)], axis=1)


def _ln_body(x_ref, m_ref, g_ref, b_ref, of_ref, op_ref):
    v = DEEPNORM_ALPHA * x_ref[...] + m_ref[...].astype(F32)
    y = _layer_norm_rows(v, g_ref[...], b_ref[...])
    of_ref[...] = y
    _store_token_rows(op_ref, _pack_halves(y))


def _residual_layer_norm(x, m, g, b):
    t, d = x.shape
    tm = min(LN_TILE, t)
    row = pl.BlockSpec((tm, d), lambda i: (i, 0))
    half = pl.BlockSpec((tm * ROW_SUBS, V7X_LANES), lambda i: (i, 0))
    vec = pl.BlockSpec((1, d), lambda i: (0, 0))
    return pl.pallas_call(
        _ln_body,
        grid=(t // tm,),
        in_specs=[row, row, vec, vec],
        out_specs=[row, half],
        out_shape=[jax.ShapeDtypeStruct((t, d), F32), jax.ShapeDtypeStruct((t * ROW_SUBS, V7X_LANES), jnp.uint32)],
        compiler_params=_params(("parallel",)),
        name="residual_layer_norm",
    )(x, m, g.reshape(1, d), b.reshape(1, d))


def _pool_body(u_ref, halo_ref, w_ref, scale_ref, o_ref, ext_ref, *, ts):
    i = pl.program_id(1)
    keep = jnp.where(i > 0, 1.0, 0.0).astype(F32)
    pos = (i * ts + lax.broadcasted_iota(jnp.int32, (ts, 1), 0) + 1).astype(F32)
    for g, win in enumerate(POOL_WINDOWS):
        cols = slice(g * POOL_GROUP_DIM, (g + 1) * POOL_GROUP_DIM)
        cur = u_ref[:, cols].astype(F32)
        ext_ref[0:HALO, :] = halo_ref[:, cols].astype(F32) * keep
        ext_ref[HALO:HALO + ts, :] = cur
        acc = cur
        for back in range(1, win):
            acc = acc + ext_ref[HALO - back:HALO - back + ts, :]
        mean = acc / jnp.minimum(pos, float(win))
        mixed = jnp.dot((mean - cur).astype(BF16), w_ref[g], preferred_element_type=F32)
        o_ref[:, cols] = (mixed * scale_ref[:, cols]).astype(o_ref.dtype)


def _pool_mixer(h, col_block, pool_w, pool_scale, batch, seq):
    ts = min(POOL_TILE, seq)
    nt = seq // ts
    hb = ts // HALO

    def cur_map(b, i):
        return (b * nt + i, col_block)

    def halo_map(b, i):
        return (jnp.maximum((b * nt + i) * hb - 1, 0), col_block)

    return pl.pallas_call(
        functools.partial(_pool_body, ts=ts),
        grid=(batch, nt),
        in_specs=[pl.BlockSpec((ts, D_POOL), cur_map),
                  pl.BlockSpec((HALO, D_POOL), halo_map),
                  pl.BlockSpec((POOL_GROUPS, POOL_GROUP_DIM, POOL_GROUP_DIM), lambda b, i: (0, 0, 0)),
                  pl.BlockSpec((1, D_POOL), lambda b, i: (0, 0))],
        out_specs=pl.BlockSpec((ts, D_POOL), lambda b, i: (b * nt + i, 0)),
        out_shape=jax.ShapeDtypeStruct((batch * seq, D_POOL), BF16),
        scratch_shapes=[pltpu.VMEM((HALO + ts, POOL_GROUP_DIM), F32)],
        compiler_params=_params(("parallel", "parallel")),
        name="pool_mixer",
    )(h, h, pool_w, pool_scale.reshape(1, D_POOL))


def _conv_silu(cur_ref, halo_ref, w_ref, b_ref, ext_ref, keep, width, ln):
    ext_ref[0:HALO, 0:width] = halo_ref[...].astype(F32) * keep
    ext_ref[HALO:HALO + ln, 0:width] = cur_ref[...].astype(F32)
    acc = b_ref[...]
    for tap in range(SSD_CONV):
        off = HALO - (SSD_CONV - 1) + tap
        acc = acc + w_ref[tap:tap + 1, :] * ext_ref[off:off + ln, 0:width]
    return acc * _sigmoid(acc)


def _ssd_body(z_ref, x_ref, xh_ref, b_ref, bh_ref, c_ref, ch_ref, dt_ref,
              cwx_ref, cbx_ref, cwb_ref, cbb_ref, cwc_ref, cbc_ref,
              dtb_ref, alog_ref, dskip_ref, nw_ref, o_ref, ext_ref, state_ref, *, ln):
    c = pl.program_id(2)
    keep = jnp.where(c > 0, 1.0, 0.0).astype(F32)

    @pl.when(c == 0)
    def _():
        state_ref[...] = jnp.zeros_like(state_ref)

    gh, p, gd = SSD_GROUP_HEADS, SSD_HEADDIM, SSD_GROUP_DIM

    xs = _conv_silu(x_ref, xh_ref, cwx_ref, cbx_ref, ext_ref, keep, gd, ln)
    bm = _conv_silu(b_ref, bh_ref, cwb_ref, cbb_ref, ext_ref, keep, SSD_STATE, ln)
    cm = _conv_silu(c_ref, ch_ref, cwc_ref, cbc_ref, ext_ref, keep, SSD_STATE, ln)

    dt = _softplus(dt_ref[...] + dtb_ref[...])
    da = dt * (-jnp.exp(alog_ref[...]))

    row = lax.broadcasted_iota(jnp.int32, (ln, ln), 0)
    col = lax.broadcasted_iota(jnp.int32, (ln, ln), 1)
    causal = row >= col
    tri = jnp.where(causal, 1.0, 0.0).astype(BF16)
    acum = _dot_with_01(da, tri, 3, v_on_left=False)
    acum_t = acum.T

    eh = lax.broadcasted_iota(jnp.int32, (V7X_LANES, gd), 0)
    ec = lax.broadcasted_iota(jnp.int32, (V7X_LANES, gd), 1)
    expand = jnp.where((ec >= eh * p) & (ec < (eh + 1) * p), 1.0, 0.0).astype(BF16)
    dt_x = _dot_with_01(dt, expand, 2)
    acum_x = _dot_with_01(acum, expand, 3)
    last_x = acum_x[ln - 1:ln, :]

    xdt = xs * dt_x
    xdt_b = xdt.astype(BF16)
    bm_b = bm.astype(BF16)
    cm_b = cm.astype(BF16)

    cb = lax.dot_general(cm_b, bm_b, (((1,), (1,)), ((), ())), preferred_element_type=F32)

    def head_diag(e, x_pair):
        seg = acum[:, e:e + 1] - acum_t[e:e + 1, :]
        decay = jnp.where(causal, jnp.exp(jnp.minimum(seg, 0.0)), 0.0)
        return jnp.dot((cb * decay).astype(BF16), x_pair, preferred_element_type=F32)

    first_half = lax.broadcasted_iota(jnp.int32, (ln, 2 * p), 1) < p
    parts = []
    for pair in range(gh // 2):
        x_pair = xdt_b[:, pair * 2 * p:(pair + 1) * 2 * p]
        parts.append(jnp.where(first_half, head_diag(2 * pair, x_pair), head_diag(2 * pair + 1, x_pair)))
    y_diag = jnp.concatenate(parts, axis=1)

    state = state_ref[...]
    y_off = jnp.dot(cm_b, state.astype(BF16), preferred_element_type=F32) * jnp.exp(acum_x)
    to_end = jnp.exp(last_x - acum_x)
    upd = jnp.dot(bm.T.astype(BF16), (xdt * to_end).astype(BF16), preferred_element_type=F32)
    state_ref[...] = state * jnp.exp(last_x) + upd

    y = y_diag + y_off + dskip_ref[...] * xs
    zf = z_ref[...].astype(F32)
    y = y * (zf * _sigmoid(zf))
    y = y * lax.rsqrt(jnp.mean(y * y, axis=-1, keepdims=True) + RMS_EPS)
    o_ref[...] = (y * nw_ref[...]).astype(o_ref.dtype)


def _ssd_mixer(h, dt_raw, conv_w, conv_b, dt_bias, a_log, d_skip, norm_w, batch, seq,
               z_block, x_block, b_block, c_block):
    ln = min(SSD_CHUNK, seq)
    nc = seq // ln
    hb = ln // HALO
    gd, gh = SSD_GROUP_DIM, SSD_GROUP_HEADS
    ngb = D_SSD // SSD_STATE

    def cur(base):
        return lambda b, g, c: (b * nc + c, base + g)

    def halo(base):
        return lambda b, g, c: (jnp.maximum((b * nc + c) * hb - 1, 0), base + g)

    grp = lambda b, g, c: (g, 0, 0)
    in_specs = [
        pl.BlockSpec((ln, gd), cur(z_block)),
        pl.BlockSpec((ln, gd), cur(x_block)), pl.BlockSpec((HALO, gd), halo(x_block)),
        pl.BlockSpec((ln, SSD_STATE), cur(b_block)), pl.BlockSpec((HALO, SSD_STATE), halo(b_block)),
        pl.BlockSpec((ln, SSD_STATE), cur(c_block)), pl.BlockSpec((HALO, SSD_STATE), halo(c_block)),
        pl.BlockSpec((None, ln, V7X_LANES), lambda b, g, c: (g, b * nc + c, 0)),
        pl.BlockSpec((SSD_CONV, gd), lambda b, g, c: (0, g)), pl.BlockSpec((1, gd), lambda b, g, c: (0, g)),
        pl.BlockSpec((SSD_CONV, SSD_STATE), lambda b, g, c: (0, ngb + g)),
        pl.BlockSpec((1, SSD_STATE), lambda b, g, c: (0, ngb + g)),
        pl.BlockSpec((SSD_CONV, SSD_STATE), lambda b, g, c: (0, ngb + SSD_GROUPS + g)),
        pl.BlockSpec((1, SSD_STATE), lambda b, g, c: (0, ngb + SSD_GROUPS + g)),
        pl.BlockSpec((None, 1, V7X_LANES), grp), pl.BlockSpec((None, 1, V7X_LANES), grp),
        pl.BlockSpec((1, gd), lambda b, g, c: (0, g)), pl.BlockSpec((1, gd), lambda b, g, c: (0, g)),
    ]
    conv_b2 = conv_b.reshape(1, -1)
    d_skip_x = jnp.repeat(d_skip.astype(F32), SSD_HEADDIM).reshape(1, D_SSD)

    def per_head(v):
        return jnp.pad(v.astype(F32).reshape(SSD_GROUPS, 1, gh), ((0, 0), (0, 0), (0, V7X_LANES - gh)))
    return pl.pallas_call(
        functools.partial(_ssd_body, ln=ln),
        grid=(batch, SSD_GROUPS, nc),
        in_specs=in_specs,
        out_specs=pl.BlockSpec((ln, gd), lambda b, g, c: (b * nc + c, g)),
        out_shape=jax.ShapeDtypeStruct((batch * seq, D_SSD), BF16),
        scratch_shapes=[pltpu.VMEM((HALO + ln, gd), F32), pltpu.VMEM((SSD_STATE, gd), F32)],
        compiler_params=_params(("parallel", "parallel", "arbitrary")),
        name="ssd_mixer",
    )(h, h, h, h, h, h, h, dt_raw,
      conv_w, conv_b2, conv_w, conv_b2, conv_w, conv_b2,
      per_head(dt_bias), per_head(a_log), d_skip_x, norm_w.reshape(1, D_SSD))


def _gconv_body(bg_ref, cg_ref, cgh_ref, xi_ref, xih_ref, w_ref, o_ref, ext_ref, *, ts):
    i = pl.program_id(1)
    keep = jnp.where(i > 0, 1.0, 0.0).astype(F32)
    ext_ref[0:HALO, :] = cgh_ref[...].astype(F32) * xih_ref[...].astype(F32) * keep
    ext_ref[HALO:HALO + ts, :] = cg_ref[...].astype(F32) * xi_ref[...].astype(F32)
    acc = jnp.zeros((ts, D_CONV), F32)
    for tap in range(CONV_WIDTH):
        off = HALO - (CONV_WIDTH - 1) + tap
        acc = acc + w_ref[tap:tap + 1, :] * ext_ref[off:off + ts, :]
    o_ref[...] = (bg_ref[...].astype(F32) * acc).astype(o_ref.dtype)


def _gated_conv(h, conv_w, batch, seq):
    ts = min(CONV_TILE, seq)
    nt = seq // ts
    hb = ts // HALO

    def cur(blk):
        return lambda b, i: (b * nt + i, blk)

    def halo(blk):
        return lambda b, i: (jnp.maximum((b * nt + i) * hb - 1, 0), blk)

    return pl.pallas_call(
        functools.partial(_gconv_body, ts=ts),
        grid=(batch, nt),
        in_specs=[pl.BlockSpec((ts, D_CONV), cur(0)),
                  pl.BlockSpec((ts, D_CONV), cur(1)), pl.BlockSpec((HALO, D_CONV), halo(1)),
                  pl.BlockSpec((ts, D_CONV), cur(2)), pl.BlockSpec((HALO, D_CONV), halo(2)),
                  pl.BlockSpec((CONV_WIDTH, D_CONV), lambda b, i: (0, 0))],
        out_specs=pl.BlockSpec((ts, D_CONV), lambda b, i: (b * nt + i, 0)),
        out_shape=jax.ShapeDtypeStruct((batch * seq, D_CONV), BF16),
        scratch_shapes=[pltpu.VMEM((HALO + ts, D_CONV), F32)],
        compiler_params=_params(("parallel", "parallel")),
        name="gated_conv",
    )(h, h, h, h, h, conv_w)


def _sb_block(q, k_ref, v_ref, start, tk, carry, acc, upper, mask):
    kt = k_ref[pl.ds(start, tk), :]
    vt = v_ref[pl.ds(start, tk), :]
    z = lax.dot_general(q, kt, (((1,), (1,)), ((), ())), preferred_element_type=F32) * (SB_HEADDIM ** -0.5)
    sp = _softplus(z)
    log_stay = -sp if mask is None else jnp.where(mask, -sp, 0.0)
    hi = log_stay.astype(BF16)
    lo = (log_stay - hi.astype(F32)).astype(BF16)
    after = (jnp.dot(hi, upper, preferred_element_type=F32)
             + jnp.dot(lo, upper, preferred_element_type=F32)) + carry
    w = jnp.exp((z - sp) + after)
    if mask is not None:
        w = jnp.where(mask, w, 0.0)
    acc = acc + jnp.dot(w.astype(BF16), vt, preferred_element_type=F32)
    carry = carry + jnp.sum(log_stay, axis=1, keepdims=True)
    return carry, acc


def _sb_body(q_ref, k_ref, v_ref, o_ref, *, tq):
    qi = pl.program_id(2)
    q = q_ref[...]
    row = lax.broadcasted_iota(jnp.int32, (tq, tq), 0)
    col = lax.broadcasted_iota(jnp.int32, (tq, tq), 1)
    upper = jnp.where(row > col, 1.0, 0.0).astype(BF16)
    diag_mask = col < row

    def tile(j, carry, acc, mask):
        return _sb_block(q, k_ref, v_ref, pl.multiple_of(j * tq, tq), tq, carry, acc, upper, mask)

    def tile_pair(j, carry, acc):
        carry, acc = tile(j, carry, acc, None)
        has_second = jnp.broadcast_to(j >= 1, (tq, tq))
        return tile(jnp.maximum(j - 1, 0), carry, acc, has_second)

    carry = jnp.zeros((tq, 1), F32)
    acc = jnp.zeros((tq, SB_HEADDIM), F32)
    carry, acc = tile(qi, carry, acc, diag_mask)
    has_prev = jnp.broadcast_to(qi >= 1, (tq, tq))
    carry, acc = tile(jnp.maximum(qi - 1, 0), carry, acc, has_prev)

    def more(state):
        j, carry, _ = state
        return jnp.logical_and(j >= 0, jnp.max(carry) > SB_UNDERFLOW_LOG)

    def step(state):
        j, carry, acc = state
        carry, acc = tile_pair(j, carry, acc)
        return j - 2, carry, acc

    _, _, acc = lax.while_loop(more, step, (qi - 2, carry, acc))
    o_ref[...] = acc.astype(o_ref.dtype)


def _sb_attention(h, q_block, k_block, v_block, batch, seq):
    tq = min(SB_TILE, seq)
    nq = seq // tq
    return pl.pallas_call(
        functools.partial(_sb_body, tq=tq),
        grid=(batch, SB_HEADS, nq),
        in_specs=[pl.BlockSpec((tq, SB_HEADDIM), lambda b, hd, i: (b * nq + i, q_block + hd)),
                  pl.BlockSpec((seq, SB_HEADDIM), lambda b, hd, i: (b, k_block + hd)),
                  pl.BlockSpec((seq, SB_HEADDIM), lambda b, hd, i: (b, v_block + hd))],
        out_specs=pl.BlockSpec((tq, SB_HEADDIM), lambda b, hd, i: (b * nq + i, hd)),
        out_shape=jax.ShapeDtypeStruct((batch * seq, D_SB), BF16),
        compiler_params=_params(("parallel", "parallel", "arbitrary")),
        name="stick_breaking_attention",
    )(h, h, h)


def _router_body(x_ref, wt_ref, b_ref, idx_ref, wgt_ref):
    w_hi, w_lo = _bf16_terms(wt_ref[...], 2)
    x_hi, x_lo = _bf16_terms(x_ref[...], 2)
    nt_dot = lambda a, b: lax.dot_general(a, b, (((1,), (1,)), ((), ())), preferred_element_type=F32)
    logits = (nt_dot(w_hi, x_hi) + (nt_dot(w_hi, x_lo) + nt_dot(w_lo, x_hi))) + b_ref[...]
    iota = lax.broadcasted_iota(jnp.int32, logits.shape, 0).astype(F32)
    vals, idxs = [], []
    for _ in range(TOP_K):
        mx = jnp.max(logits, axis=0, keepdims=True)
        ix = jnp.min(jnp.where(logits == mx, iota, float(N_EXPERTS)), axis=0, keepdims=True)
        vals.append(mx)
        idxs.append(ix)
        logits = jnp.where(iota == ix, -jnp.inf, logits)
    exps = [jnp.exp(v - vals[0]) for v in vals]
    denom = exps[0] + exps[1] + exps[2] + exps[3]
    for k in range(TOP_K):
        idx_ref[k:k + 1, :] = idxs[k].astype(jnp.int32)
        wgt_ref[k:k + 1, :] = exps[k] / denom


def _router(x, router_w, router_b):
    t, d = x.shape
    tm = min(ROUTER_TILE, t)
    out = pl.BlockSpec((TOP_K, tm), lambda i: (0, i))
    return pl.pallas_call(
        _router_body,
        grid=(t // tm,),
        in_specs=[pl.BlockSpec((tm, d), lambda i: (i, 0)),
                  pl.BlockSpec((N_EXPERTS, d), lambda i: (0, 0)),
                  pl.BlockSpec((N_EXPERTS, 1), lambda i: (0, 0))],
        out_specs=[out, out],
        out_shape=[jax.ShapeDtypeStruct((TOP_K, t), jnp.int32), jax.ShapeDtypeStruct((TOP_K, t), F32)],
        compiler_params=_params(("parallel",)),
        name="moe_router",
    )(x, router_w.T, router_b.reshape(N_EXPERTS, 1))


def _routing_tables(idx, tm):
    t = idx.shape[1]
    pairs = TOP_K * t
    rows = pairs + N_EXPERTS * tm
    ntiles = rows // tm
    e_flat = idx.reshape(pairs)
    onehot = (e_flat[:, None] == jnp.arange(N_EXPERTS, dtype=jnp.int32)[None, :]).astype(jnp.int32)
    csum = jnp.cumsum(onehot, axis=0)
    rank = jnp.sum((csum - onehot) * onehot, axis=1)
    counts = csum[-1]
    padded = ((counts + tm - 1) // tm) * tm
    gend = jnp.cumsum(padded)
    gstart = gend - padded
    pos = (gstart[e_flat] + rank).astype(jnp.int32)
    token = (jnp.arange(pairs, dtype=jnp.int32) % t)
    tok_of_row = jnp.zeros((rows,), jnp.int32).at[pos].set(token)
    tile_start = jnp.arange(ntiles, dtype=jnp.int32) * tm
    tile_expert = jnp.sum((tile_start[:, None] >= gend[None, :]).astype(jnp.int32), axis=1)
    tile_expert = jnp.minimum(tile_expert, N_EXPERTS - 1)
    tile_valid = (tile_start < gend[-1]).astype(jnp.int32)
    pos_tok = pos.reshape(TOP_K, t).T.reshape(pairs)
    return tok_of_row, pos_tok, tile_expert, tile_valid


def _dispatch_body(tok_ref, x_hbm, o_ref, buf, sem, *, batch):
    i = pl.program_id(0)
    nt = pl.num_programs(0)

    def issue(step, slot):
        base = step * batch

        def body(r, carry):
            src = pl.multiple_of(tok_ref[base + r] * ROW_SUBS, ROW_SUBS)
            dst = pl.multiple_of(r * ROW_SUBS, ROW_SUBS)
            pltpu.make_async_copy(x_hbm.at[pl.ds(src, ROW_SUBS), :], buf.at[slot, pl.ds(dst, ROW_SUBS), :],
                                  sem.at[slot]).start()
            return carry

        lax.fori_loop(0, batch, body, 0, unroll=MOE_ISSUE_UNROLL)

    @pl.when(i == 0)
    def _():
        issue(0, 0)

    @pl.when(i + 1 < nt)
    def _():
        issue(i + 1, (i + 1) % 2)

    slot = i % 2
    pltpu.make_async_copy(x_hbm.at[pl.ds(0, batch * ROW_SUBS), :], buf.at[slot], sem.at[slot]).wait()
    o_ref[...] = buf[slot]


def _dispatch(x_packed, tok_of_row, batch):
    rows = tok_of_row.shape[0]
    grid_spec = pltpu.PrefetchScalarGridSpec(
        num_scalar_prefetch=1,
        grid=(rows // batch,),
        in_specs=[pl.BlockSpec(memory_space=pl.ANY)],
        out_specs=pl.BlockSpec((batch * ROW_SUBS, V7X_LANES), lambda i, tok: (i, 0)),
        scratch_shapes=[pltpu.VMEM((2, batch * ROW_SUBS, V7X_LANES), jnp.uint32), pltpu.SemaphoreType.DMA((2,))],
    )
    return pl.pallas_call(
        functools.partial(_dispatch_body, batch=batch),
        grid_spec=grid_spec,
        out_shape=jax.ShapeDtypeStruct((rows * ROW_SUBS, V7X_LANES), jnp.uint32),
        compiler_params=_params(("arbitrary",)),
        name="moe_dispatch",
    )(tok_of_row, x_packed)


def _expert_changed(te_ref, i):
    return jnp.logical_or(i == 0, te_ref[i] != te_ref[jnp.maximum(i - 1, 0)])


def _up_body(te_ref, tv_ref, x_ref, w1_ref, b1_ref, o_ref, w1b_ref):
    i = pl.program_id(0)
    tm = o_ref.shape[0]
    half = ROW_SUBS * V7X_LANES

    @pl.when(_expert_changed(te_ref, i))
    def _():
        w1b_ref[...] = w1_ref[...].astype(BF16)

    @pl.when(tv_ref[i] == 1)
    def _():
        low, high = _unpack_halves(_load_token_rows(x_ref, tm))
        hh = (jnp.dot(low.astype(BF16), w1b_ref[0:half, :], preferred_element_type=F32)
              + jnp.dot(high.astype(BF16), w1b_ref[half:2 * half, :], preferred_element_type=F32)) + b1_ref[...]
        glu = jnp.minimum(hh[:, :D_EXPERT], SWIGLU_LIMIT)
        lin = jnp.clip(hh[:, D_EXPERT:], -SWIGLU_LIMIT, SWIGLU_LIMIT)
        o_ref[...] = (glu * _sigmoid(SWIGLU_ALPHA * glu) * (lin + 1.0)).astype(o_ref.dtype)

    @pl.when(tv_ref[i] == 0)
    def _():
        o_ref[...] = jnp.zeros_like(o_ref)


def _down_body(te_ref, tv_ref, a_ref, w2_ref, b2_ref, o_ref, w2b_ref):
    i = pl.program_id(0)

    @pl.when(_expert_changed(te_ref, i))
    def _():
        w2b_ref[...] = w2_ref[...].astype(BF16)

    @pl.when(tv_ref[i] == 1)
    def _():
        out = jnp.dot(a_ref[...], w2b_ref[...], preferred_element_type=F32) + b2_ref[...]
        o_ref[...] = _pack_halves(out)

    @pl.when(tv_ref[i] == 0)
    def _():
        o_ref[...] = jnp.zeros_like(o_ref)


def _experts(x_sorted, w1, b1, w2, b2, tile_expert, tile_valid, tm):
    rows = x_sorted.shape[0] // ROW_SUBS
    half = ROW_SUBS * V7X_LANES
    d = 2 * half
    ntiles = rows // tm
    by_expert = lambda i, te, tv: (te[i], 0, 0)
    by_tile = lambda i, te, tv: (i, 0)
    packed_tile = pl.BlockSpec((tm * ROW_SUBS, V7X_LANES), by_tile)
    act = pl.pallas_call(
        _up_body,
        grid_spec=pltpu.PrefetchScalarGridSpec(
            num_scalar_prefetch=2,
            grid=(ntiles,),
            in_specs=[packed_tile,
                      pl.BlockSpec((None, d, 2 * D_EXPERT), by_expert),
                      pl.BlockSpec((None, 1, 2 * D_EXPERT), by_expert)],
            out_specs=pl.BlockSpec((tm, D_EXPERT), by_tile),
            scratch_shapes=[pltpu.VMEM((d, 2 * D_EXPERT), BF16)],
        ),
        out_shape=jax.ShapeDtypeStruct((rows, D_EXPERT), BF16),
        compiler_params=_params(("arbitrary",)),
        name="moe_up",
    )(tile_expert, tile_valid, x_sorted, w1, b1.reshape(N_EXPERTS, 1, -1))
    return pl.pallas_call(
        _down_body,
        grid_spec=pltpu.PrefetchScalarGridSpec(
            num_scalar_prefetch=2,
            grid=(ntiles,),
            in_specs=[pl.BlockSpec((tm, D_EXPERT), by_tile),
                      pl.BlockSpec((None, D_EXPERT, d), by_expert),
                      pl.BlockSpec((None, 1, d), by_expert)],
            out_specs=pl.BlockSpec((tm, half), by_tile),
            scratch_shapes=[pltpu.VMEM((D_EXPERT, d), BF16)],
        ),
        out_shape=jax.ShapeDtypeStruct((rows, half), jnp.uint32),
        compiler_params=_params(("arbitrary",)),
        name="moe_down",
    )(tile_expert, tile_valid, act, w2, b2.reshape(N_EXPERTS, 1, -1))


def _combine_body(pos_ref, y_hbm, x_ref, wgt_ref, g_ref, b_ref, of_ref, *rest, tm, emit_bf16):
    ob_ref = rest[0] if emit_bf16 else None
    ybuf, sem = rest[-2:]
    i = pl.program_id(0)
    nt = pl.num_programs(0)

    def row_copy(src, slot, k, r):
        return pltpu.make_async_copy(y_hbm.at[pl.ds(src, 1), :], ybuf.at[slot, k, pl.ds(r, 1), :], sem.at[slot])

    def issue(tile, slot):
        base = tile * tm * TOP_K

        def body(r, carry):
            for k in range(TOP_K):
                row_copy(pos_ref[base + r * TOP_K + k], slot, k, r).start()
            return carry

        lax.fori_loop(0, tm, body, 0, unroll=4)

    @pl.when(i == 0)
    def _():
        issue(0, 0)

    @pl.when(i + 1 < nt)
    def _():
        issue(i + 1, (i + 1) % 2)

    slot = i % 2
    for k in range(TOP_K):
        pltpu.make_async_copy(y_hbm.at[pl.ds(0, tm), :], ybuf.at[slot, k], sem.at[slot]).wait()
    wgt = wgt_ref[...]
    moe_low, moe_high = None, None
    for k in range(TOP_K):
        low, high = _unpack_halves(ybuf[slot, k])
        gk = wgt[:, k:k + 1]
        moe_low = gk * low if k == 0 else moe_low + gk * low
        moe_high = gk * high if k == 0 else moe_high + gk * high
    moe = jnp.concatenate([moe_low, moe_high], axis=1)
    y = _layer_norm_rows(DEEPNORM_ALPHA * x_ref[...] + moe, g_ref[...], b_ref[...])
    of_ref[...] = y
    if emit_bf16:
        ob_ref[...] = y.astype(BF16)


def _combine_layer_norm(x, y_rows, pos_tok, wgt_tok, g, b, tm, emit_bf16):
    t, d = x.shape
    half = d // 2
    row = lambda i, pos: (i, 0)
    vec = lambda i, pos: (0, 0)
    out_specs = [pl.BlockSpec((tm, d), row)]
    out_shape = [jax.ShapeDtypeStruct((t, d), F32)]
    if emit_bf16:
        out_specs.append(pl.BlockSpec((tm, d), row))
        out_shape.append(jax.ShapeDtypeStruct((t, d), BF16))
    grid_spec = pltpu.PrefetchScalarGridSpec(
        num_scalar_prefetch=1,
        grid=(t // tm,),
        in_specs=[pl.BlockSpec(memory_space=pl.ANY),
                  pl.BlockSpec((tm, d), row), pl.BlockSpec((tm, TOP_K), row),
                  pl.BlockSpec((1, d), vec), pl.BlockSpec((1, d), vec)],
        out_specs=out_specs,
        scratch_shapes=[pltpu.VMEM((2, TOP_K, tm, half), jnp.uint32), pltpu.SemaphoreType.DMA((2,))],
    )
    return pl.pallas_call(
        functools.partial(_combine_body, tm=tm, emit_bf16=emit_bf16),
        grid_spec=grid_spec,
        out_shape=out_shape,
        compiler_params=_params(("arbitrary",)),
        name="moe_combine_layer_norm",
    )(pos_tok, y_rows, x, wgt_tok, g.reshape(1, d), b.reshape(1, d))


def _moe_block(x_f32, x_packed, router_w, router_b, w1, b1, w2, b2, g, b, emit_bf16):
    t = x_f32.shape[0]
    tm = min(MOE_TILE, t)
    idx, wgt = _router(x_f32, router_w, router_b)
    tok_of_row, pos_tok, tile_expert, tile_valid = _routing_tables(idx, tm)
    x_sorted = _dispatch(x_packed, tok_of_row, tm)
    y_rows = _experts(x_sorted, w1, b1, w2, b2, tile_expert, tile_valid, tm)
    return _combine_layer_norm(x_f32, y_rows, pos_tok, wgt.T, g, b, min(COMBINE_TILE, t), emit_bf16)


def _layer0_mixer(x_f32, x_b16, batch, seq, w_in, conv_w, conv_b, dt_bias, a_log, d_skip, norm_w,
                  pool_w, pool_scale, w_out, ln_g, ln_b):
    n_ssd = 2 * D_SSD + 2 * SSD_BC
    h_u = _matmul(x_b16, w_in, BF16, 1024, 512, 0, D_POOL)
    h = _matmul(x_b16, w_in, BF16, 1024, 512, D_POOL, n_ssd)
    w_dt = jnp.pad(w_in[:, D_POOL + n_ssd:], ((0, 0), (0, V7X_LANES - SSD_HEADS)))
    dt_raw = _matmul(x_b16, w_dt, F32, 1024, V7X_LANES)[:, :SSD_HEADS]
    t = batch * seq
    dt_raw = dt_raw.reshape(t, SSD_GROUPS, SSD_GROUP_HEADS).transpose(1, 0, 2)
    dt_raw = jnp.pad(dt_raw, ((0, 0), (0, 0), (0, V7X_LANES - SSD_GROUP_HEADS)))
    y_ssd = _ssd_mixer(h, dt_raw, conv_w, conv_b, dt_bias, a_log, d_skip, norm_w, batch, seq,
                       z_block=0, x_block=D_SSD // SSD_GROUP_DIM,
                       b_block=2 * D_SSD // SSD_STATE, c_block=(2 * D_SSD + SSD_BC) // SSD_STATE)
    y_pool = _pool_mixer(h_u, 0, pool_w.astype(BF16), pool_scale, batch, seq)
    m = _matmul_cat(y_pool, y_ssd, w_out, BF16, 1024, 1024, D_POOL)
    return _residual_layer_norm(x_f32, m, ln_g, ln_b)


def _layer1_mixer(x_f32, x_b16, batch, seq, w_in, conv_w, w_out, ln_g, ln_b):
    h = _matmul(x_b16, w_in, BF16, 1024, 512)
    y_conv = _gated_conv(h, conv_w, batch, seq)
    qb = 3 * D_CONV // SB_HEADDIM
    y_sb = _sb_attention(h, qb, qb + SB_HEADS, qb + 2 * SB_HEADS, batch, seq)
    m = _matmul_cat(y_conv, y_sb, w_out, BF16, 1024, 1024, D_CONV)
    return _residual_layer_norm(x_f32, m, ln_g, ln_b)


def kernel(x, l0_w_in, l0_conv_w, l0_conv_b, l0_dt_bias, l0_a_log, l0_d_skip, l0_ssm_norm_w, l0_pool_w, l0_pool_scale, l0_w_out, l0_ln_mix_g, l0_ln_mix_b, l0_router_w, l0_router_b, l0_w1, l0_b1, l0_w2, l0_b2, l0_ln_ffn_g, l0_ln_ffn_b, l1_w_in, l1_conv_w, l1_w_out, l1_ln_mix_g, l1_ln_mix_b, l1_router_w, l1_router_b, l1_w1, l1_b1, l1_w2, l1_b2, l1_ln_ffn_g, l1_ln_ffn_b):
    batch, seq, d = x.shape
    xf = x.reshape(batch * seq, d)
    xb = xf.astype(BF16)
    xf, xp = _layer0_mixer(xf, xb, batch, seq, l0_w_in, l0_conv_w, l0_conv_b, l0_dt_bias, l0_a_log, l0_d_skip,
                           l0_ssm_norm_w, l0_pool_w, l0_pool_scale, l0_w_out, l0_ln_mix_g, l0_ln_mix_b)
    xf, xb = _moe_block(xf, xp, l0_router_w, l0_router_b, l0_w1, l0_b1, l0_w2, l0_b2, l0_ln_ffn_g, l0_ln_ffn_b,
                        emit_bf16=True)
    xf, xp = _layer1_mixer(xf, xb, batch, seq, l1_w_in, l1_conv_w, l1_w_out, l1_ln_mix_g, l1_ln_mix_b)
    (xf,) = _moe_block(xf, xp, l1_router_w, l1_router_b, l1_w1, l1_b1, l1_w2, l1_b2, l1_ln_ffn_g, l1_ln_ffn_b,
                       emit_bf16=False)
    return xf.reshape(batch, seq, d)
```

```python
import functools

import jax
import jax.numpy as jnp
from jax import lax
from jax.experimental import pallas as pl
from jax.experimental.pallas import tpu as pltpu

F32 = jnp.float32
BF16 = jnp.bfloat16

D_MODEL = 4096
DEPTH = 2
DEEPNORM_ALPHA = (2.0 * DEPTH) ** 0.25
LN_EPS = 1e-5
RMS_EPS = 1e-5

POOL_WINDOWS = (2, 4, 8, 16)
POOL_GROUPS = 4
D_POOL = D_MODEL // 2
POOL_GROUP_DIM = D_POOL // POOL_GROUPS

SSD_HEADDIM = 64
D_SSD = (3 * D_MODEL) // 2
SSD_HEADS = D_SSD // SSD_HEADDIM
SSD_GROUPS = 8
SSD_STATE = 128
SSD_CONV = 4
SSD_GROUP_HEADS = SSD_HEADS // SSD_GROUPS
SSD_GROUP_DIM = D_SSD // SSD_GROUPS
SSD_BC = SSD_GROUPS * SSD_STATE

D_CONV = D_MODEL // 2
CONV_WIDTH = 3
SB_HEADS = 16
SB_HEADDIM = 128
D_SB = SB_HEADS * SB_HEADDIM

N_EXPERTS = 32
TOP_K = 4
D_EXPERT = 512
SWIGLU_LIMIT = 7.0
SWIGLU_ALPHA = 1.702

V7X_LANES = 128
V7X_BF16_SUBLANES = 16
V7X_VMEM_LIMIT = 56 * 1024 * 1024
HALO = V7X_BF16_SUBLANES

SSD_CHUNK = 256
POOL_TILE = 512
CONV_TILE = 512
LN_TILE = 256
ROUTER_TILE = 512
SB_TILE = 256
SB_UNDERFLOW_LOG = -110.0
MOE_TILE = 512
MOE_ISSUE_UNROLL = 8
DMA_QUEUES = 2
COMBINE_TILE = 128


def _params(semantics):
    return pltpu.CompilerParams(dimension_semantics=semantics, vmem_limit_bytes=V7X_VMEM_LIMIT)


def _softplus(x):
    return jnp.maximum(x, 0.0) + jnp.log(1.0 + jnp.exp(-jnp.abs(x)))


def _sigmoid(x):
    return 1.0 / (1.0 + jnp.exp(-x))


def _bf16_terms(v, terms):
    out = []
    for _ in range(terms - 1):
        piece = v.astype(BF16)
        out.append(piece)
        v = v - piece.astype(F32)
    out.append(v.astype(BF16))
    return out


def _dot_with_01(v, mat01, terms, v_on_left=True):
    parts = _bf16_terms(v, terms)
    if v_on_left:
        dots = [jnp.dot(p, mat01, preferred_element_type=F32) for p in parts]
    else:
        dots = [jnp.dot(mat01, p, preferred_element_type=F32) for p in parts]
    total = dots[0]
    for d in dots[1:]:
        total = total + d
    return total


def _mm_body(a_ref, w_ref, o_ref, wb_ref):
    @pl.when(pl.program_id(1) == 0)
    def _():
        wb_ref[...] = w_ref[...].astype(BF16)

    o_ref[...] = jnp.dot(a_ref[...], wb_ref[...], preferred_element_type=F32).astype(o_ref.dtype)


def _matmul(a, w, out_dtype, tm, tn, col0=0, ncols=None):
    m, kdim = a.shape
    n = w.shape[1] - col0 if ncols is None else ncols
    tm, tn = min(tm, m), min(tn, n)
    assert m % tm == 0 and n % tn == 0 and col0 % tn == 0 and w.shape[0] == kdim
    cb0 = col0 // tn
    return pl.pallas_call(
        _mm_body,
        grid=(n // tn, m // tm),
        in_specs=[pl.BlockSpec((tm, kdim), lambda j, i: (i, 0)),
                  pl.BlockSpec((kdim, tn), lambda j, i: (0, cb0 + j))],
        out_specs=pl.BlockSpec((tm, tn), lambda j, i: (i, j)),
        out_shape=jax.ShapeDtypeStruct((m, n), out_dtype),
        scratch_shapes=[pltpu.VMEM((kdim, tn), BF16)],
        compiler_params=_params(("parallel", "arbitrary")),
        name="matmul",
    )(a, w)


def _mm_cat_body(a1_ref, a2_ref, w_ref, o_ref, acc_ref, *, n1, nk):
    k = pl.program_id(2)

    def accumulate(a_ref):
        part = jnp.dot(a_ref[...], w_ref[...].astype(BF16), preferred_element_type=F32)

        @pl.when(k == 0)
        def _():
            acc_ref[...] = part

        @pl.when(k > 0)
        def _():
            acc_ref[...] += part

    @pl.when(k < n1)
    def _():
        accumulate(a1_ref)

    @pl.when(k >= n1)
    def _():
        accumulate(a2_ref)

    @pl.when(k == nk - 1)
    def _():
        o_ref[...] = acc_ref[...].astype(o_ref.dtype)


def _matmul_cat(a1, a2, w, out_dtype, tm, tn, tk):
    m, k1 = a1.shape
    k2 = a2.shape[1]
    n = w.shape[1]
    assert k1 % tk == 0 and k2 % tk == 0 and m % tm == 0 and n % tn == 0 and w.shape[0] == k1 + k2
    n1, nk = k1 // tk, (k1 + k2) // tk
    return pl.pallas_call(
        functools.partial(_mm_cat_body, n1=n1, nk=nk),
        grid=(n // tn, m // tm, nk),
        in_specs=[pl.BlockSpec((tm, tk), lambda j, i, k: (i, jnp.minimum(k, n1 - 1))),
                  pl.BlockSpec((tm, tk), lambda j, i, k: (i, jnp.maximum(k - n1, 0))),
                  pl.BlockSpec((tk, tn), lambda j, i, k: (k, j))],
        out_specs=pl.BlockSpec((tm, tn), lambda j, i, k: (i, j)),
        out_shape=jax.ShapeDtypeStruct((m, n), out_dtype),
        scratch_shapes=[pltpu.VMEM((tm, tn), F32)],
        compiler_params=_params(("parallel", "parallel", "arbitrary")),
        name="matmul_cat",
    )(a1, a2, w)


def _layer_norm_rows(v, g, b):
    mu = jnp.mean(v, axis=-1, keepdims=True)
    vc = v - mu
    var = jnp.mean(vc * vc, axis=-1, keepdims=True)
    return vc * lax.rsqrt(var + LN_EPS) * g + b


def _bf16_bits(v):
    return lax.bitcast_convert_type(v.astype(BF16).astype(F32), jnp.uint32)


def _pack_halves(v):
    n = v.shape[1] // 2
    return _bf16_bits(v[:, n:]) | (_bf16_bits(v[:, :n]) >> 16)


def _unpack_halves(words):
    low = lax.bitcast_convert_type(words << 16, F32)
    high = lax.bitcast_convert_type(words & jnp.uint32(0xFFFF0000), F32)
    return low, high


ROW_SUBS = (D_MODEL // 2) // V7X_LANES


def _store_token_rows(ref, words):
    n = words.shape[0]
    for s in range(ROW_SUBS):
        ref[pl.ds(s, n, stride=ROW_SUBS), :] = words[:, s * V7X_LANES:(s + 1) * V7X_LANES]


def _load_token_rows(ref, n):
    return jnp.concatenate([ref[pl.ds(s, n, stride=ROW_SUBS), :] for s in range(ROW_SUBS)], axis=1)


def _ln_body(x_ref, m_ref, g_ref, b_ref, of_ref, op_ref):
    v = DEEPNORM_ALPHA * x_ref[...] + m_ref[...].astype(F32)
    y = _layer_norm_rows(v, g_ref[...], b_ref[...])
    of_ref[...] = y
    _store_token_rows(op_ref, _pack_halves(y))


def _residual_layer_norm(x, m, g, b):
    t, d = x.shape
    tm = min(LN_TILE, t)
    row = pl.BlockSpec((tm, d), lambda i: (i, 0))
    half = pl.BlockSpec((tm * ROW_SUBS, V7X_LANES), lambda i: (i, 0))
    vec = pl.BlockSpec((1, d), lambda i: (0, 0))
    return pl.pallas_call(
        _ln_body,
        grid=(t // tm,),
        in_specs=[row, row, vec, vec],
        out_specs=[row, half],
        out_shape=[jax.ShapeDtypeStruct((t, d), F32), jax.ShapeDtypeStruct((t * ROW_SUBS, V7X_LANES), jnp.uint32)],
        compiler_params=_params(("parallel",)),
        name="residual_layer_norm",
    )(x, m, g.reshape(1, d), b.reshape(1, d))


def _pool_body(u_ref, halo_ref, w_ref, scale_ref, o_ref, ext_ref, *, ts):
    i = pl.program_id(1)
    keep = jnp.where(i > 0, 1.0, 0.0).astype(F32)
    pos = (i * ts + lax.broadcasted_iota(jnp.int32, (ts, 1), 0) + 1).astype(F32)
    for g, win in enumerate(POOL_WINDOWS):
        cols = slice(g * POOL_GROUP_DIM, (g + 1) * POOL_GROUP_DIM)
        cur = u_ref[:, cols].astype(F32)
        ext_ref[0:HALO, :] = halo_ref[:, cols].astype(F32) * keep
        ext_ref[HALO:HALO + ts, :] = cur
        acc = cur
        for back in range(1, win):
            acc = acc + ext_ref[HALO - back:HALO - back + ts, :]
        mean = acc / jnp.minimum(pos, float(win))
        mixed = jnp.dot((mean - cur).astype(BF16), w_ref[g], preferred_element_type=F32)
        o_ref[:, cols] = (mixed * scale_ref[:, cols]).astype(o_ref.dtype)


def _pool_mixer(h, col_block, pool_w, pool_scale, batch, seq):
    ts = min(POOL_TILE, seq)
    nt = seq // ts
    hb = ts // HALO

    def cur_map(b, i):
        return (b * nt + i, col_block)

    def halo_map(b, i):
        return (jnp.maximum((b * nt + i) * hb - 1, 0), col_block)

    return pl.pallas_call(
        functools.partial(_pool_body, ts=ts),
        grid=(batch, nt),
        in_specs=[pl.BlockSpec((ts, D_POOL), cur_map),
                  pl.BlockSpec((HALO, D_POOL), halo_map),
                  pl.BlockSpec((POOL_GROUPS, POOL_GROUP_DIM, POOL_GROUP_DIM), lambda b, i: (0, 0, 0)),
                  pl.BlockSpec((1, D_POOL), lambda b, i: (0, 0))],
        out_specs=pl.BlockSpec((ts, D_POOL), lambda b, i: (b * nt + i, 0)),
        out_shape=jax.ShapeDtypeStruct((batch * seq, D_POOL), BF16),
        scratch_shapes=[pltpu.VMEM((HALO + ts, POOL_GROUP_DIM), F32)],
        compiler_params=_params(("parallel", "parallel")),
        name="pool_mixer",
    )(h, h, pool_w, pool_scale.reshape(1, D_POOL))


def _conv_shift_matrix(ln):
    rows = (SSD_CONV - 1) * ln
    r = lax.broadcasted_iota(jnp.int32, (rows, HALO + ln), 0)
    j = lax.broadcasted_iota(jnp.int32, (rows, HALO + ln), 1)
    hit = None
    for tap in range(SSD_CONV - 1):
        in_tap = (r >= tap * ln) & (r < (tap + 1) * ln) & (j == r - tap * ln + HALO - (SSD_CONV - 1) + tap)
        hit = in_tap if hit is None else hit | in_tap
    return jnp.where(hit, 1.0, 0.0).astype(BF16)


def _conv_silu(cur_ref, halo_ref, w_ref, b_ref, ext_ref, shift, keep, width, ln):
    ext_ref[0:HALO, 0:width] = halo_ref[...] * keep.astype(BF16)
    ext_ref[HALO:HALO + ln, 0:width] = cur_ref[...]
    shifted = jnp.dot(shift, ext_ref[:, 0:width], preferred_element_type=F32)
    acc = b_ref[...] + w_ref[SSD_CONV - 1:SSD_CONV, :] * cur_ref[...].astype(F32)
    for tap in range(SSD_CONV - 1):
        acc = acc + w_ref[tap:tap + 1, :] * shifted[tap * ln:(tap + 1) * ln, :]
    return acc * _sigmoid(acc)


def _ssd_body(z_ref, x_ref, xh_ref, b_ref, bh_ref, c_ref, ch_ref, dt_ref,
              cwx_ref, cbx_ref, cwb_ref, cbb_ref, cwc_ref, cbc_ref,
              dtb_ref, alog_ref, dskip_ref, nw_ref, o_ref, ext_ref, state_ref, *, ln):
    c = pl.program_id(2)
    keep = jnp.where(c > 0, 1.0, 0.0).astype(F32)

    @pl.when(c == 0)
    def _():
        state_ref[...] = jnp.zeros_like(state_ref)

    gh, p, gd = SSD_GROUP_HEADS, SSD_HEADDIM, SSD_GROUP_DIM

    shift = _conv_shift_matrix(ln)
    xs = _conv_silu(x_ref, xh_ref, cwx_ref, cbx_ref, ext_ref, shift, keep, gd, ln)
    bm = _conv_silu(b_ref, bh_ref, cwb_ref, cbb_ref, ext_ref, shift, keep, SSD_STATE, ln)
    cm = _conv_silu(c_ref, ch_ref, cwc_ref, cbc_ref, ext_ref, shift, keep, SSD_STATE, ln)

    dt = _softplus(dt_ref[...] + dtb_ref[...])
    da = dt * (-jnp.exp(alog_ref[...]))

    row = lax.broadcasted_iota(jnp.int32, (ln, ln), 0)
    col = lax.broadcasted_iota(jnp.int32, (ln, ln), 1)
    causal = row >= col
    tri = jnp.where(causal, 1.0, 0.0).astype(BF16)
    acum = _dot_with_01(da, tri, 3, v_on_left=False)
    acum_t = acum.T

    eh = lax.broadcasted_iota(jnp.int32, (V7X_LANES, gd), 0)
    ec = lax.broadcasted_iota(jnp.int32, (V7X_LANES, gd), 1)
    expand = jnp.where((ec >= eh * p) & (ec < (eh + 1) * p), 1.0, 0.0).astype(BF16)
    dt_x = _dot_with_01(dt, expand, 2)
    acum_x = _dot_with_01(acum, expand, 3)
    last_x = acum_x[ln - 1:ln, :]

    xdt = xs * dt_x
    xdt_b = xdt.astype(BF16)
    bm_b = bm.astype(BF16)
    cm_b = cm.astype(BF16)

    cb = lax.dot_general(cm_b, bm_b, (((1,), (1,)), ((), ())), preferred_element_type=F32)

    def head_diag(e, x_pair):
        seg = acum[:, e:e + 1] - acum_t[e:e + 1, :]
        decay = jnp.where(causal, jnp.exp(jnp.minimum(seg, 0.0)), 0.0)
        return jnp.dot((cb * decay).astype(BF16), x_pair, preferred_element_type=F32)

    first_half = lax.broadcasted_iota(jnp.int32, (ln, 2 * p), 1) < p
    parts = []
    for pair in range(gh // 2):
        x_pair = xdt_b[:, pair * 2 * p:(pair + 1) * 2 * p]
        parts.append(jnp.where(first_half, head_diag(2 * pair, x_pair), head_diag(2 * pair + 1, x_pair)))
    y_diag = jnp.concatenate(parts, axis=1)

    state = state_ref[...]
    y_off = jnp.dot(cm_b, state.astype(BF16), preferred_element_type=F32) * jnp.exp(acum_x)
    to_end = jnp.exp(last_x - acum_x)
    upd = jnp.dot(bm.T.astype(BF16), (xdt * to_end).astype(BF16), preferred_element_type=F32)
    state_ref[...] = state * jnp.exp(last_x) + upd

    y = y_diag + y_off + dskip_ref[...] * xs
    zf = z_ref[...].astype(F32)
    y = y * (zf * _sigmoid(zf))
    y = y * lax.rsqrt(jnp.mean(y * y, axis=-1, keepdims=True) + RMS_EPS)
    o_ref[...] = (y * nw_ref[...]).astype(o_ref.dtype)


def _ssd_mixer(h, dt_raw, conv_w, conv_b, dt_bias, a_log, d_skip, norm_w, batch, seq,
               z_block, x_block, b_block, c_block):
    ln = min(SSD_CHUNK, seq)
    nc = seq // ln
    hb = ln // HALO
    gd, gh = SSD_GROUP_DIM, SSD_GROUP_HEADS
    ngb = D_SSD // SSD_STATE

    def cur(base):
        return lambda b, g, c: (b * nc + c, base + g)

    def halo(base):
        return lambda b, g, c: (jnp.maximum((b * nc + c) * hb - 1, 0), base + g)

    grp = lambda b, g, c: (g, 0, 0)
    in_specs = [
        pl.BlockSpec((ln, gd), cur(z_block)),
        pl.BlockSpec((ln, gd), cur(x_block)), pl.BlockSpec((HALO, gd), halo(x_block)),
        pl.BlockSpec((ln, SSD_STATE), cur(b_block)), pl.BlockSpec((HALO, SSD_STATE), halo(b_block)),
        pl.BlockSpec((ln, SSD_STATE), cur(c_block)), pl.BlockSpec((HALO, SSD_STATE), halo(c_block)),
        pl.BlockSpec((None, ln, V7X_LANES), lambda b, g, c: (g, b * nc + c, 0)),
        pl.BlockSpec((SSD_CONV, gd), lambda b, g, c: (0, g)), pl.BlockSpec((1, gd), lambda b, g, c: (0, g)),
        pl.BlockSpec((SSD_CONV, SSD_STATE), lambda b, g, c: (0, ngb + g)),
        pl.BlockSpec((1, SSD_STATE), lambda b, g, c: (0, ngb + g)),
        pl.BlockSpec((SSD_CONV, SSD_STATE), lambda b, g, c: (0, ngb + SSD_GROUPS + g)),
        pl.BlockSpec((1, SSD_STATE), lambda b, g, c: (0, ngb + SSD_GROUPS + g)),
        pl.BlockSpec((None, 1, V7X_LANES), grp), pl.BlockSpec((None, 1, V7X_LANES), grp),
        pl.BlockSpec((1, gd), lambda b, g, c: (0, g)), pl.BlockSpec((1, gd), lambda b, g, c: (0, g)),
    ]
    conv_b2 = conv_b.reshape(1, -1)
    d_skip_x = jnp.repeat(d_skip.astype(F32), SSD_HEADDIM).reshape(1, D_SSD)

    def per_head(v):
        return jnp.pad(v.astype(F32).reshape(SSD_GROUPS, 1, gh), ((0, 0), (0, 0), (0, V7X_LANES - gh)))
    return pl.pallas_call(
        functools.partial(_ssd_body, ln=ln),
        grid=(batch, SSD_GROUPS, nc),
        in_specs=in_specs,
        out_specs=pl.BlockSpec((ln, gd), lambda b, g, c: (b * nc + c, g)),
        out_shape=jax.ShapeDtypeStruct((batch * seq, D_SSD), BF16),
        scratch_shapes=[pltpu.VMEM((HALO + ln, gd), BF16), pltpu.VMEM((SSD_STATE, gd), F32)],
        compiler_params=_params(("parallel", "parallel", "arbitrary")),
        name="ssd_mixer",
    )(h, h, h, h, h, h, h, dt_raw,
      conv_w, conv_b2, conv_w, conv_b2, conv_w, conv_b2,
      per_head(dt_bias), per_head(a_log), d_skip_x, norm_w.reshape(1, D_SSD))


def _gconv_body(bg_ref, cg_ref, cgh_ref, xi_ref, xih_ref, w_ref, o_ref, ext_ref, *, ts):
    i = pl.program_id(1)
    keep = jnp.where(i > 0, 1.0, 0.0).astype(F32)
    ext_ref[0:HALO, :] = cgh_ref[...].astype(F32) * xih_ref[...].astype(F32) * keep
    ext_ref[HALO:HALO + ts, :] = cg_ref[...].astype(F32) * xi_ref[...].astype(F32)
    acc = jnp.zeros((ts, D_CONV), F32)
    for tap in range(CONV_WIDTH):
        off = HALO - (CONV_WIDTH - 1) + tap
        acc = acc + w_ref[tap:tap + 1, :] * ext_ref[off:off + ts, :]
    o_ref[...] = (bg_ref[...].astype(F32) * acc).astype(o_ref.dtype)


def _gated_conv(h, conv_w, batch, seq):
    ts = min(CONV_TILE, seq)
    nt = seq // ts
    hb = ts // HALO

    def cur(blk):
        return lambda b, i: (b * nt + i, blk)

    def halo(blk):
        return lambda b, i: (jnp.maximum((b * nt + i) * hb - 1, 0), blk)

    return pl.pallas_call(
        functools.partial(_gconv_body, ts=ts),
        grid=(batch, nt),
        in_specs=[pl.BlockSpec((ts, D_CONV), cur(0)),
                  pl.BlockSpec((ts, D_CONV), cur(1)), pl.BlockSpec((HALO, D_CONV), halo(1)),
                  pl.BlockSpec((ts, D_CONV), cur(2)), pl.BlockSpec((HALO, D_CONV), halo(2)),
                  pl.BlockSpec((CONV_WIDTH, D_CONV), lambda b, i: (0, 0))],
        out_specs=pl.BlockSpec((ts, D_CONV), lambda b, i: (b * nt + i, 0)),
        out_shape=jax.ShapeDtypeStruct((batch * seq, D_CONV), BF16),
        scratch_shapes=[pltpu.VMEM((HALO + ts, D_CONV), F32)],
        compiler_params=_params(("parallel", "parallel")),
        name="gated_conv",
    )(h, h, h, h, h, conv_w)


def _sb_block(q, k_ref, v_ref, start, tk, carry, acc, upper, mask):
    kt = k_ref[pl.ds(start, tk), :]
    vt = v_ref[pl.ds(start, tk), :]
    z = lax.dot_general(q, kt, (((1,), (1,)), ((), ())), preferred_element_type=F32) * (SB_HEADDIM ** -0.5)
    sp = _softplus(z)
    log_stay = -sp if mask is None else jnp.where(mask, -sp, 0.0)
    hi = log_stay.astype(BF16)
    lo = (log_stay - hi.astype(F32)).astype(BF16)
    after = (jnp.dot(hi, upper, preferred_element_type=F32)
             + jnp.dot(lo, upper, preferred_element_type=F32)) + carry
    w = jnp.exp((z - sp) + after)
    if mask is not None:
        w = jnp.where(mask, w, 0.0)
    acc = acc + jnp.dot(w.astype(BF16), vt, preferred_element_type=F32)
    carry = carry + jnp.sum(log_stay, axis=1, keepdims=True)
    return carry, acc


def _sb_body(q_ref, k_ref, v_ref, o_ref, *, tq):
    qi = pl.program_id(2)
    q = q_ref[...]
    row = lax.broadcasted_iota(jnp.int32, (tq, tq), 0)
    col = lax.broadcasted_iota(jnp.int32, (tq, tq), 1)
    upper = jnp.where(row > col, 1.0, 0.0).astype(BF16)
    diag_mask = col < row

    def tile(j, carry, acc, mask):
        return _sb_block(q, k_ref, v_ref, pl.multiple_of(j * tq, tq), tq, carry, acc, upper, mask)

    def tile_pair(j, carry, acc):
        carry, acc = tile(j, carry, acc, None)
        has_second = jnp.broadcast_to(j >= 1, (tq, tq))
        return tile(jnp.maximum(j - 1, 0), carry, acc, has_second)

    carry = jnp.zeros((tq, 1), F32)
    acc = jnp.zeros((tq, SB_HEADDIM), F32)
    carry, acc = tile(qi, carry, acc, diag_mask)
    has_prev = jnp.broadcast_to(qi >= 1, (tq, tq))
    carry, acc = tile(jnp.maximum(qi - 1, 0), carry, acc, has_prev)

    def more(state):
        j, carry, _ = state
        return jnp.logical_and(j >= 0, jnp.max(carry) > SB_UNDERFLOW_LOG)

    def step(state):
        j, carry, acc = state
        carry, acc = tile_pair(j, carry, acc)
        return j - 2, carry, acc

    _, _, acc = lax.while_loop(more, step, (qi - 2, carry, acc))
    o_ref[...] = acc.astype(o_ref.dtype)


def _sb_attention(h, q_block, k_block, v_block, batch, seq):
    tq = min(SB_TILE, seq)
    nq = seq // tq
    return pl.pallas_call(
        functools.partial(_sb_body, tq=tq),
        grid=(batch, SB_HEADS, nq),
        in_specs=[pl.BlockSpec((tq, SB_HEADDIM), lambda b, hd, i: (b * nq + i, q_block + hd)),
                  pl.BlockSpec((seq, SB_HEADDIM), lambda b, hd, i: (b, k_block + hd)),
                  pl.BlockSpec((seq, SB_HEADDIM), lambda b, hd, i: (b, v_block + hd))],
        out_specs=pl.BlockSpec((tq, SB_HEADDIM), lambda b, hd, i: (b * nq + i, hd)),
        out_shape=jax.ShapeDtypeStruct((batch * seq, D_SB), BF16),
        compiler_params=_params(("parallel", "parallel", "arbitrary")),
        name="stick_breaking_attention",
    )(h, h, h)


def _router_body(x_ref, wt_ref, b_ref, idx_ref, wgt_ref):
    w_hi, w_lo = _bf16_terms(wt_ref[...], 2)
    x_hi, x_lo = _bf16_terms(x_ref[...], 2)
    nt_dot = lambda a, b: lax.dot_general(a, b, (((1,), (1,)), ((), ())), preferred_element_type=F32)
    logits = (nt_dot(w_hi, x_hi) + (nt_dot(w_hi, x_lo) + nt_dot(w_lo, x_hi))) + b_ref[...]
    iota = lax.broadcasted_iota(jnp.int32, logits.shape, 0).astype(F32)
    vals, idxs = [], []
    for _ in range(TOP_K):
        mx = jnp.max(logits, axis=0, keepdims=True)
        ix = jnp.min(jnp.where(logits == mx, iota, float(N_EXPERTS)), axis=0, keepdims=True)
        vals.append(mx)
        idxs.append(ix)
        logits = jnp.where(iota == ix, -jnp.inf, logits)
    exps = [jnp.exp(v - vals[0]) for v in vals]
    denom = exps[0] + exps[1] + exps[2] + exps[3]
    for k in range(TOP_K):
        idx_ref[k:k + 1, :] = idxs[k].astype(jnp.int32)
        wgt_ref[k:k + 1, :] = exps[k] / denom


def _router(x, router_w, router_b):
    t, d = x.shape
    tm = min(ROUTER_TILE, t)
    out = pl.BlockSpec((TOP_K, tm), lambda i: (0, i))
    return pl.pallas_call(
        _router_body,
        grid=(t // tm,),
        in_specs=[pl.BlockSpec((tm, d), lambda i: (i, 0)),
                  pl.BlockSpec((N_EXPERTS, d), lambda i: (0, 0)),
                  pl.BlockSpec((N_EXPERTS, 1), lambda i: (0, 0))],
        out_specs=[out, out],
        out_shape=[jax.ShapeDtypeStruct((TOP_K, t), jnp.int32), jax.ShapeDtypeStruct((TOP_K, t), F32)],
        compiler_params=_params(("parallel",)),
        name="moe_router",
    )(x, router_w.T, router_b.reshape(N_EXPERTS, 1))


def _routing_tables(idx, tm):
    t = idx.shape[1]
    pairs = TOP_K * t
    rows = pairs + N_EXPERTS * tm
    ntiles = rows // tm
    e_flat = idx.reshape(pairs)
    onehot = (e_flat[:, None] == jnp.arange(N_EXPERTS, dtype=jnp.int32)[None, :]).astype(jnp.int32)
    csum = jnp.cumsum(onehot, axis=0)
    rank = jnp.sum((csum - onehot) * onehot, axis=1)
    counts = csum[-1]
    padded = ((counts + tm - 1) // tm) * tm
    gend = jnp.cumsum(padded)
    gstart = gend - padded
    pos = (gstart[e_flat] + rank).astype(jnp.int32)
    token = (jnp.arange(pairs, dtype=jnp.int32) % t)
    tok_of_row = jnp.zeros((rows,), jnp.int32).at[pos].set(token)
    tile_start = jnp.arange(ntiles, dtype=jnp.int32) * tm
    tile_expert = jnp.sum((tile_start[:, None] >= gend[None, :]).astype(jnp.int32), axis=1)
    tile_expert = jnp.minimum(tile_expert, N_EXPERTS - 1)
    tile_valid = (tile_start < gend[-1]).astype(jnp.int32)
    pos_tok = pos.reshape(TOP_K, t).T.reshape(pairs)
    return tok_of_row, pos_tok, tile_expert, tile_valid


def _dispatch_body(tok_ref, x_hbm, o_ref, buf, sem, *, batch):
    i = pl.program_id(0)
    nt = pl.num_programs(0)

    def issue(step, slot):
        base = step * batch

        def body(q, carry):
            for queue in range(DMA_QUEUES):
                r = q * DMA_QUEUES + queue
                src = pl.multiple_of(tok_ref[base + r] * ROW_SUBS, ROW_SUBS)
                dst = pl.multiple_of(r * ROW_SUBS, ROW_SUBS)
                pltpu.make_async_copy(x_hbm.at[pl.ds(src, ROW_SUBS), :], buf.at[slot, pl.ds(dst, ROW_SUBS), :],
                                      sem.at[slot]).start(priority=queue)
            return carry

        lax.fori_loop(0, batch // DMA_QUEUES, body, 0, unroll=MOE_ISSUE_UNROLL // DMA_QUEUES)

    @pl.when(i == 0)
    def _():
        issue(0, 0)

    @pl.when(i + 1 < nt)
    def _():
        issue(i + 1, (i + 1) % 2)

    slot = i % 2
    pltpu.make_async_copy(x_hbm.at[pl.ds(0, batch * ROW_SUBS), :], buf.at[slot], sem.at[slot]).wait()
    o_ref[...] = buf[slot]


def _dispatch(x_packed, tok_of_row, batch):
    rows = tok_of_row.shape[0]
    grid_spec = pltpu.PrefetchScalarGridSpec(
        num_scalar_prefetch=1,
        grid=(rows // batch,),
        in_specs=[pl.BlockSpec(memory_space=pl.ANY)],
        out_specs=pl.BlockSpec((batch * ROW_SUBS, V7X_LANES), lambda i, tok: (i, 0)),
        scratch_shapes=[pltpu.VMEM((2, batch * ROW_SUBS, V7X_LANES), jnp.uint32), pltpu.SemaphoreType.DMA((2,))],
    )
    return pl.pallas_call(
        functools.partial(_dispatch_body, batch=batch),
        grid_spec=grid_spec,
        out_shape=jax.ShapeDtypeStruct((rows * ROW_SUBS, V7X_LANES), jnp.uint32),
        compiler_params=_params(("arbitrary",)),
        name="moe_dispatch",
    )(tok_of_row, x_packed)


def _expert_changed(te_ref, i):
    return jnp.logical_or(i == 0, te_ref[i] != te_ref[jnp.maximum(i - 1, 0)])


def _up_body(te_ref, tv_ref, x_ref, w1_ref, b1_ref, o_ref, w1b_ref):
    i = pl.program_id(0)
    tm = o_ref.shape[0]
    half = ROW_SUBS * V7X_LANES

    @pl.when(_expert_changed(te_ref, i))
    def _():
        w1b_ref[...] = w1_ref[...].astype(BF16)

    @pl.when(tv_ref[i] == 1)
    def _():
        low, high = _unpack_halves(_load_token_rows(x_ref, tm))
        hh = (jnp.dot(low.astype(BF16), w1b_ref[0:half, :], preferred_element_type=F32)
              + jnp.dot(high.astype(BF16), w1b_ref[half:2 * half, :], preferred_element_type=F32)) + b1_ref[...]
        glu = jnp.minimum(hh[:, :D_EXPERT], SWIGLU_LIMIT)
        lin = jnp.clip(hh[:, D_EXPERT:], -SWIGLU_LIMIT, SWIGLU_LIMIT)
        o_ref[...] = (glu * _sigmoid(SWIGLU_ALPHA * glu) * (lin + 1.0)).astype(o_ref.dtype)

    @pl.when(tv_ref[i] == 0)
    def _():
        o_ref[...] = jnp.zeros_like(o_ref)


def _down_body(te_ref, tv_ref, a_ref, w2_ref, b2_ref, o_ref, w2b_ref):
    i = pl.program_id(0)

    @pl.when(_expert_changed(te_ref, i))
    def _():
        w2b_ref[...] = w2_ref[...].astype(BF16)

    @pl.when(tv_ref[i] == 1)
    def _():
        out = jnp.dot(a_ref[...], w2b_ref[...], preferred_element_type=F32) + b2_ref[...]
        o_ref[...] = _pack_halves(out)

    @pl.when(tv_ref[i] == 0)
    def _():
        o_ref[...] = jnp.zeros_like(o_ref)


def _experts(x_sorted, w1, b1, w2, b2, tile_expert, tile_valid, tm):
    rows = x_sorted.shape[0] // ROW_SUBS
    half = ROW_SUBS * V7X_LANES
    d = 2 * half
    ntiles = rows // tm
    by_expert = lambda i, te, tv: (te[i], 0, 0)
    by_tile = lambda i, te, tv: (i, 0)
    packed_tile = pl.BlockSpec((tm * ROW_SUBS, V7X_LANES), by_tile)
    act = pl.pallas_call(
        _up_body,
        grid_spec=pltpu.PrefetchScalarGridSpec(
            num_scalar_prefetch=2,
            grid=(ntiles,),
            in_specs=[packed_tile,
                      pl.BlockSpec((None, d, 2 * D_EXPERT), by_expert),
                      pl.BlockSpec((None, 1, 2 * D_EXPERT), by_expert)],
            out_specs=pl.BlockSpec((tm, D_EXPERT), by_tile),
            scratch_shapes=[pltpu.VMEM((d, 2 * D_EXPERT), BF16)],
        ),
        out_shape=jax.ShapeDtypeStruct((rows, D_EXPERT), BF16),
        compiler_params=_params(("arbitrary",)),
        name="moe_up",
    )(tile_expert, tile_valid, x_sorted, w1, b1.reshape(N_EXPERTS, 1, -1))
    return pl.pallas_call(
        _down_body,
        grid_spec=pltpu.PrefetchScalarGridSpec(
            num_scalar_prefetch=2,
            grid=(ntiles,),
            in_specs=[pl.BlockSpec((tm, D_EXPERT), by_tile),
                      pl.BlockSpec((None, D_EXPERT, d), by_expert),
                      pl.BlockSpec((None, 1, d), by_expert)],
            out_specs=pl.BlockSpec((tm, half), by_tile),
            scratch_shapes=[pltpu.VMEM((D_EXPERT, d), BF16)],
        ),
        out_shape=jax.ShapeDtypeStruct((rows, half), jnp.uint32),
        compiler_params=_params(("arbitrary",)),
        name="moe_down",
    )(tile_expert, tile_valid, act, w2, b2.reshape(N_EXPERTS, 1, -1))


def _combine_body(pos_ref, y_hbm, x_ref, wgt_ref, g_ref, b_ref, of_ref, *rest, tm, emit_bf16):
    ob_ref = rest[0] if emit_bf16 else None
    ybuf, sem = rest[-2:]
    i = pl.program_id(0)
    nt = pl.num_programs(0)

    def row_copy(src, slot, k, r):
        return pltpu.make_async_copy(y_hbm.at[pl.ds(src, 1), :], ybuf.at[slot, k, pl.ds(r, 1), :], sem.at[slot])

    def issue(tile, slot):
        base = tile * tm * TOP_K

        def body(r, carry):
            for k in range(TOP_K):
                row_copy(pos_ref[base + r * TOP_K + k], slot, k, r).start(priority=k % DMA_QUEUES)
            return carry

        lax.fori_loop(0, tm, body, 0, unroll=4)

    @pl.when(i == 0)
    def _():
        issue(0, 0)

    @pl.when(i + 1 < nt)
    def _():
        issue(i + 1, (i + 1) % 2)

    slot = i % 2
    for k in range(TOP_K):
        pltpu.make_async_copy(y_hbm.at[pl.ds(0, tm), :], ybuf.at[slot, k], sem.at[slot]).wait()
    wgt = wgt_ref[...]
    moe_low, moe_high = None, None
    for k in range(TOP_K):
        low, high = _unpack_halves(ybuf[slot, k])
        gk = wgt[:, k:k + 1]
        moe_low = gk * low if k == 0 else moe_low + gk * low
        moe_high = gk * high if k == 0 else moe_high + gk * high
    moe = jnp.concatenate([moe_low, moe_high], axis=1)
    y = _layer_norm_rows(DEEPNORM_ALPHA * x_ref[...] + moe, g_ref[...], b_ref[...])
    of_ref[...] = y
    if emit_bf16:
        ob_ref[...] = y.astype(BF16)


def _combine_layer_norm(x, y_rows, pos_tok, wgt_tok, g, b, tm, emit_bf16):
    t, d = x.shape
    half = d // 2
    row = lambda i, pos: (i, 0)
    vec = lambda i, pos: (0, 0)
    out_specs = [pl.BlockSpec((tm, d), row)]
    out_shape = [jax.ShapeDtypeStruct((t, d), F32)]
    if emit_bf16:
        out_specs.append(pl.BlockSpec((tm, d), row))
        out_shape.append(jax.ShapeDtypeStruct((t, d), BF16))
    grid_spec = pltpu.PrefetchScalarGridSpec(
        num_scalar_prefetch=1,
        grid=(t // tm,),
        in_specs=[pl.BlockSpec(memory_space=pl.ANY),
                  pl.BlockSpec((tm, d), row), pl.BlockSpec((tm, TOP_K), row),
                  pl.BlockSpec((1, d), vec), pl.BlockSpec((1, d), vec)],
        out_specs=out_specs,
        scratch_shapes=[pltpu.VMEM((2, TOP_K, tm, half), jnp.uint32), pltpu.SemaphoreType.DMA((2,))],
    )
    return pl.pallas_call(
        functools.partial(_combine_body, tm=tm, emit_bf16=emit_bf16),
        grid_spec=grid_spec,
        out_shape=out_shape,
        compiler_params=_params(("arbitrary",)),
        name="moe_combine_layer_norm",
    )(pos_tok, y_rows, x, wgt_tok, g.reshape(1, d), b.reshape(1, d))


def _moe_block(x_f32, x_packed, router_w, router_b, w1, b1, w2, b2, g, b, emit_bf16):
    t = x_f32.shape[0]
    tm = min(MOE_TILE, t)
    idx, wgt = _router(x_f32, router_w, router_b)
    tok_of_row, pos_tok, tile_expert, tile_valid = _routing_tables(idx, tm)
    x_sorted = _dispatch(x_packed, tok_of_row, tm)
    y_rows = _experts(x_sorted, w1, b1, w2, b2, tile_expert, tile_valid, tm)
    return _combine_layer_norm(x_f32, y_rows, pos_tok, wgt.T, g, b, min(COMBINE_TILE, t), emit_bf16)


def _layer0_mixer(x_f32, x_b16, batch, seq, w_in, conv_w, conv_b, dt_bias, a_log, d_skip, norm_w,
                  pool_w, pool_scale, w_out, ln_g, ln_b):
    n_ssd = 2 * D_SSD + 2 * SSD_BC
    h_u = _matmul(x_b16, w_in, BF16, 1024, 512, 0, D_POOL)
    h = _matmul(x_b16, w_in, BF16, 1024, 512, D_POOL, n_ssd)
    w_dt = jnp.pad(w_in[:, D_POOL + n_ssd:], ((0, 0), (0, V7X_LANES - SSD_HEADS)))
    dt_raw = _matmul(x_b16, w_dt, F32, 1024, V7X_LANES)[:, :SSD_HEADS]
    t = batch * seq
    dt_raw = dt_raw.reshape(t, SSD_GROUPS, SSD_GROUP_HEADS).transpose(1, 0, 2)
    dt_raw = jnp.pad(dt_raw, ((0, 0), (0, 0), (0, V7X_LANES - SSD_GROUP_HEADS)))
    y_ssd = _ssd_mixer(h, dt_raw, conv_w, conv_b, dt_bias, a_log, d_skip, norm_w, batch, seq,
                       z_block=0, x_block=D_SSD // SSD_GROUP_DIM,
                       b_block=2 * D_SSD // SSD_STATE, c_block=(2 * D_SSD + SSD_BC) // SSD_STATE)
    y_pool = _pool_mixer(h_u, 0, pool_w.astype(BF16), pool_scale, batch, seq)
    m = _matmul_cat(y_pool, y_ssd, w_out, BF16, 1024, 1024, D_POOL)
    return _residual_layer_norm(x_f32, m, ln_g, ln_b)


def _layer1_mixer(x_f32, x_b16, batch, seq, w_in, conv_w, w_out, ln_g, ln_b):
    h = _matmul(x_b16, w_in, BF16, 1024, 512)
    y_conv = _gated_conv(h, conv_w, batch, seq)
    qb = 3 * D_CONV // SB_HEADDIM
    y_sb = _sb_attention(h, qb, qb + SB_HEADS, qb + 2 * SB_HEADS, batch, seq)
    m = _matmul_cat(y_conv, y_sb, w_out, BF16, 1024, 1024, D_CONV)
    return _residual_layer_norm(x_f32, m, ln_g, ln_b)


def kernel(x, l0_w_in, l0_conv_w, l0_conv_b, l0_dt_bias, l0_a_log, l0_d_skip, l0_ssm_norm_w, l0_pool_w, l0_pool_scale, l0_w_out, l0_ln_mix_g, l0_ln_mix_b, l0_router_w, l0_router_b, l0_w1, l0_b1, l0_w2, l0_b2, l0_ln_ffn_g, l0_ln_ffn_b, l1_w_in, l1_conv_w, l1_w_out, l1_ln_mix_g, l1_ln_mix_b, l1_router_w, l1_router_b, l1_w1, l1_b1, l1_w2, l1_b2, l1_ln_ffn_g, l1_ln_ffn_b):
    batch, seq, d = x.shape
    xf = x.reshape(batch * seq, d)
    xb = xf.astype(BF16)
    xf, xp = _layer0_mixer(xf, xb, batch, seq, l0_w_in, l0_conv_w, l0_conv_b, l0_dt_bias, l0_a_log, l0_d_skip,
                           l0_ssm_norm_w, l0_pool_w, l0_pool_scale, l0_w_out, l0_ln_mix_g, l0_ln_mix_b)
    xf, xb = _moe_block(xf, xp, l0_router_w, l0_router_b, l0_w1, l0_b1, l0_w2, l0_b2, l0_ln_ffn_g, l0_ln_ffn_b,
                        emit_bf16=True)
    xf, xp = _layer1_mixer(xf, xb, batch, seq, l1_w_in, l1_conv_w, l1_w_out, l1_ln_mix_g, l1_ln_mix_b)
    (xf,) = _moe_block(xf, xp, l1_router_w, l1_router_b, l1_w1, l1_b1, l1_w2, l1_b2, l1_ln_ffn_g, l1_ln_ffn_b,
                       emit_bf16=False)
    return xf.reshape(batch, seq, d)
```

```python
import functools

import jax
import jax.numpy as jnp
from jax import lax
from jax.experimental import pallas as pl
from jax.experimental.pallas import tpu as pltpu

F32 = jnp.float32
BF16 = jnp.bfloat16

D_MODEL = 4096
DEPTH = 2
DEEPNORM_ALPHA = (2.0 * DEPTH) ** 0.25
LN_EPS = 1e-5
RMS_EPS = 1e-5

POOL_WINDOWS = (2, 4, 8, 16)
POOL_GROUPS = 4
D_POOL = D_MODEL // 2
POOL_GROUP_DIM = D_POOL // POOL_GROUPS

SSD_HEADDIM = 64
D_SSD = (3 * D_MODEL) // 2
SSD_HEADS = D_SSD // SSD_HEADDIM
SSD_GROUPS = 8
SSD_STATE = 128
SSD_CONV = 4
SSD_GROUP_HEADS = SSD_HEADS // SSD_GROUPS
SSD_GROUP_DIM = D_SSD // SSD_GROUPS
SSD_BC = SSD_GROUPS * SSD_STATE

D_CONV = D_MODEL // 2
CONV_WIDTH = 3
SB_HEADS = 16
SB_HEADDIM = 128
D_SB = SB_HEADS * SB_HEADDIM

N_EXPERTS = 32
TOP_K = 4
D_EXPERT = 512
SWIGLU_LIMIT = 7.0
SWIGLU_ALPHA = 1.702

V7X_LANES = 128
V7X_BF16_SUBLANES = 16
V7X_VMEM_LIMIT = 56 * 1024 * 1024
HALO = V7X_BF16_SUBLANES

SSD_CHUNK = 256
POOL_TILE = 512
CONV_TILE = 512
LN_TILE = 256
ROUTER_TILE = 512
SB_TILE = 256
SB_UNDERFLOW_LOG = -110.0
MOE_TILE = 512
MOE_ISSUE_UNROLL = 8
DMA_QUEUES = 2
COMBINE_TILE = 128


def _params(semantics):
    return pltpu.CompilerParams(dimension_semantics=semantics, vmem_limit_bytes=V7X_VMEM_LIMIT)


def _softplus(x):
    return jnp.maximum(x, 0.0) + jnp.log(1.0 + jnp.exp(-jnp.abs(x)))


def _sigmoid(x):
    return 1.0 / (1.0 + jnp.exp(-x))


def _bf16_terms(v, terms):
    out = []
    for _ in range(terms - 1):
        piece = v.astype(BF16)
        out.append(piece)
        v = v - piece.astype(F32)
    out.append(v.astype(BF16))
    return out


def _dot_with_01(v, mat01, terms, v_on_left=True):
    parts = _bf16_terms(v, terms)
    if v_on_left:
        dots = [jnp.dot(p, mat01, preferred_element_type=F32) for p in parts]
    else:
        dots = [jnp.dot(mat01, p, preferred_element_type=F32) for p in parts]
    total = dots[0]
    for d in dots[1:]:
        total = total + d
    return total


def _mm_body(a_ref, w_ref, o_ref, wb_ref):
    @pl.when(pl.program_id(1) == 0)
    def _():
        wb_ref[...] = w_ref[...].astype(BF16)

    o_ref[...] = jnp.dot(a_ref[...], wb_ref[...], preferred_element_type=F32).astype(o_ref.dtype)


def _matmul(a, w, out_dtype, tm, tn, col0=0, ncols=None):
    m, kdim = a.shape
    n = w.shape[1] - col0 if ncols is None else ncols
    tm, tn = min(tm, m), min(tn, n)
    assert m % tm == 0 and n % tn == 0 and col0 % tn == 0 and w.shape[0] == kdim
    cb0 = col0 // tn
    return pl.pallas_call(
        _mm_body,
        grid=(n // tn, m // tm),
        in_specs=[pl.BlockSpec((tm, kdim), lambda j, i: (i, 0)),
                  pl.BlockSpec((kdim, tn), lambda j, i: (0, cb0 + j))],
        out_specs=pl.BlockSpec((tm, tn), lambda j, i: (i, j)),
        out_shape=jax.ShapeDtypeStruct((m, n), out_dtype),
        scratch_shapes=[pltpu.VMEM((kdim, tn), BF16)],
        compiler_params=_params(("parallel", "arbitrary")),
        name="matmul",
    )(a, w)


def _mm_cat_body(a1_ref, a2_ref, w_ref, o_ref, acc_ref, *, n1, nk):
    k = pl.program_id(2)

    def accumulate(a_ref):
        part = jnp.dot(a_ref[...], w_ref[...].astype(BF16), preferred_element_type=F32)

        @pl.when(k == 0)
        def _():
            acc_ref[...] = part

        @pl.when(k > 0)
        def _():
            acc_ref[...] += part

    @pl.when(k < n1)
    def _():
        accumulate(a1_ref)

    @pl.when(k >= n1)
    def _():
        accumulate(a2_ref)

    @pl.when(k == nk - 1)
    def _():
        o_ref[...] = acc_ref[...].astype(o_ref.dtype)


def _matmul_cat(a1, a2, w, out_dtype, tm, tn, tk):
    m, k1 = a1.shape
    k2 = a2.shape[1]
    n = w.shape[1]
    assert k1 % tk == 0 and k2 % tk == 0 and m % tm == 0 and n % tn == 0 and w.shape[0] == k1 + k2
    n1, nk = k1 // tk, (k1 + k2) // tk
    return pl.pallas_call(
        functools.partial(_mm_cat_body, n1=n1, nk=nk),
        grid=(n // tn, m // tm, nk),
        in_specs=[pl.BlockSpec((tm, tk), lambda j, i, k: (i, jnp.minimum(k, n1 - 1))),
                  pl.BlockSpec((tm, tk), lambda j, i, k: (i, jnp.maximum(k - n1, 0))),
                  pl.BlockSpec((tk, tn), lambda j, i, k: (k, j))],
        out_specs=pl.BlockSpec((tm, tn), lambda j, i, k: (i, j)),
        out_shape=jax.ShapeDtypeStruct((m, n), out_dtype),
        scratch_shapes=[pltpu.VMEM((tm, tn), F32)],
        compiler_params=_params(("parallel", "parallel", "arbitrary")),
        name="matmul_cat",
    )(a1, a2, w)


def _layer_norm_rows(v, g, b):
    mu = jnp.mean(v, axis=-1, keepdims=True)
    vc = v - mu
    var = jnp.mean(vc * vc, axis=-1, keepdims=True)
    return vc * lax.rsqrt(var + LN_EPS) * g + b


def _bf16_bits(v):
    return lax.bitcast_convert_type(v.astype(BF16).astype(F32), jnp.uint32)


def _pack_halves(v):
    n = v.shape[1] // 2
    return _bf16_bits(v[:, n:]) | (_bf16_bits(v[:, :n]) >> 16)


def _unpack_halves(words):
    low = lax.bitcast_convert_type(words << 16, F32)
    high = lax.bitcast_convert_type(words & jnp.uint32(0xFFFF0000), F32)
    return low, high


ROW_SUBS = (D_MODEL // 2) // V7X_LANES


def _store_token_rows(ref, words):
    n = words.shape[0]
    for s in range(ROW_SUBS):
        ref[pl.ds(s, n, stride=ROW_SUBS), :] = words[:, s * V7X_LANES:(s + 1) * V7X_LANES]


def _load_token_rows(ref, n):
    return jnp.concatenate([ref[pl.ds(s, n, stride=ROW_SUBS), :] for s in range(ROW_SUBS)], axis=1)


def _ln_body(x_ref, m_ref, g_ref, b_ref, of_ref, op_ref):
    v = DEEPNORM_ALPHA * x_ref[...] + m_ref[...].astype(F32)
    y = _layer_norm_rows(v, g_ref[...], b_ref[...])
    of_ref[...] = y
    _store_token_rows(op_ref, _pack_halves(y))


def _residual_layer_norm(x, m, g, b):
    t, d = x.shape
    tm = min(LN_TILE, t)
    row = pl.BlockSpec((tm, d), lambda i: (i, 0))
    half = pl.BlockSpec((tm * ROW_SUBS, V7X_LANES), lambda i: (i, 0))
    vec = pl.BlockSpec((1, d), lambda i: (0, 0))
    return pl.pallas_call(
        _ln_body,
        grid=(t // tm,),
        in_specs=[row, row, vec, vec],
        out_specs=[row, half],
        out_shape=[jax.ShapeDtypeStruct((t, d), F32), jax.ShapeDtypeStruct((t * ROW_SUBS, V7X_LANES), jnp.uint32)],
        compiler_params=_params(("parallel",)),
        name="residual_layer_norm",
    )(x, m, g.reshape(1, d), b.reshape(1, d))


def _pool_body(u_ref, halo_ref, w_ref, scale_ref, o_ref, ext_ref, *, ts):
    i = pl.program_id(1)
    keep = jnp.where(i > 0, 1.0, 0.0).astype(F32)
    pos = (i * ts + lax.broadcasted_iota(jnp.int32, (ts, 1), 0) + 1).astype(F32)
    for g, win in enumerate(POOL_WINDOWS):
        cols = slice(g * POOL_GROUP_DIM, (g + 1) * POOL_GROUP_DIM)
        cur = u_ref[:, cols].astype(F32)
        ext_ref[0:HALO, :] = halo_ref[:, cols].astype(F32) * keep
        ext_ref[HALO:HALO + ts, :] = cur
        acc = cur
        for back in range(1, win):
            acc = acc + ext_ref[HALO - back:HALO - back + ts, :]
        mean = acc / jnp.minimum(pos, float(win))
        mixed = jnp.dot((mean - cur).astype(BF16), w_ref[g], preferred_element_type=F32)
        o_ref[:, cols] = (mixed * scale_ref[:, cols]).astype(o_ref.dtype)


def _pool_mixer(h, col_block, pool_w, pool_scale, batch, seq):
    ts = min(POOL_TILE, seq)
    nt = seq // ts
    hb = ts // HALO

    def cur_map(b, i):
        return (b * nt + i, col_block)

    def halo_map(b, i):
        return (jnp.maximum((b * nt + i) * hb - 1, 0), col_block)

    return pl.pallas_call(
        functools.partial(_pool_body, ts=ts),
        grid=(batch, nt),
        in_specs=[pl.BlockSpec((ts, D_POOL), cur_map),
                  pl.BlockSpec((HALO, D_POOL), halo_map),
                  pl.BlockSpec((POOL_GROUPS, POOL_GROUP_DIM, POOL_GROUP_DIM), lambda b, i: (0, 0, 0)),
                  pl.BlockSpec((1, D_POOL), lambda b, i: (0, 0))],
        out_specs=pl.BlockSpec((ts, D_POOL), lambda b, i: (b * nt + i, 0)),
        out_shape=jax.ShapeDtypeStruct((batch * seq, D_POOL), BF16),
        scratch_shapes=[pltpu.VMEM((HALO + ts, POOL_GROUP_DIM), F32)],
        compiler_params=_params(("parallel", "parallel")),
        name="pool_mixer",
    )(h, h, pool_w, pool_scale.reshape(1, D_POOL))


def _conv_shift_matrix(ln):
    rows = (SSD_CONV - 1) * ln
    r = lax.broadcasted_iota(jnp.int32, (rows, HALO + ln), 0)
    j = lax.broadcasted_iota(jnp.int32, (rows, HALO + ln), 1)
    hit = None
    for tap in range(SSD_CONV - 1):
        in_tap = (r >= tap * ln) & (r < (tap + 1) * ln) & (j == r - tap * ln + HALO - (SSD_CONV - 1) + tap)
        hit = in_tap if hit is None else hit | in_tap
    return jnp.where(hit, 1.0, 0.0).astype(BF16)


def _conv_silu(cur_ref, halo_ref, w_ref, b_ref, ext_ref, shift, keep, width, ln):
    ext_ref[0:HALO, 0:width] = halo_ref[...] * keep.astype(BF16)
    ext_ref[HALO:HALO + ln, 0:width] = cur_ref[...]
    shifted = jnp.dot(shift, ext_ref[:, 0:width], preferred_element_type=F32)
    acc = b_ref[...] + w_ref[SSD_CONV - 1:SSD_CONV, :] * cur_ref[...].astype(F32)
    for tap in range(SSD_CONV - 1):
        acc = acc + w_ref[tap:tap + 1, :] * shifted[tap * ln:(tap + 1) * ln, :]
    return acc * _sigmoid(acc)


def _ssd_body(z_ref, x_ref, xh_ref, b_ref, bh_ref, c_ref, ch_ref, dt_ref,
              cwx_ref, cbx_ref, cwb_ref, cbb_ref, cwc_ref, cbc_ref,
              dtb_ref, alog_ref, dskip_ref, nw_ref, o_ref, ext_ref, state_ref, *, ln):
    c = pl.program_id(2)
    keep = jnp.where(c > 0, 1.0, 0.0).astype(F32)

    @pl.when(c == 0)
    def _():
        state_ref[...] = jnp.zeros_like(state_ref)

    gh, p, gd = SSD_GROUP_HEADS, SSD_HEADDIM, SSD_GROUP_DIM

    shift = _conv_shift_matrix(ln)
    xs = _conv_silu(x_ref, xh_ref, cwx_ref, cbx_ref, ext_ref, shift, keep, gd, ln)
    bm = _conv_silu(b_ref, bh_ref, cwb_ref, cbb_ref, ext_ref, shift, keep, SSD_STATE, ln)
    cm = _conv_silu(c_ref, ch_ref, cwc_ref, cbc_ref, ext_ref, shift, keep, SSD_STATE, ln)

    dt = _softplus(dt_ref[...] + dtb_ref[...])
    da = dt * (-jnp.exp(alog_ref[...]))

    row = lax.broadcasted_iota(jnp.int32, (ln, ln), 0)
    col = lax.broadcasted_iota(jnp.int32, (ln, ln), 1)
    causal = row >= col
    tri = jnp.where(causal, 1.0, 0.0).astype(BF16)
    acum = _dot_with_01(da, tri, 3, v_on_left=False)
    acum_t = acum.T

    eh = lax.broadcasted_iota(jnp.int32, (V7X_LANES, gd), 0)
    ec = lax.broadcasted_iota(jnp.int32, (V7X_LANES, gd), 1)
    expand = jnp.where((ec >= eh * p) & (ec < (eh + 1) * p), 1.0, 0.0).astype(BF16)
    dt_x = _dot_with_01(dt, expand, 2)
    acum_x = _dot_with_01(acum, expand, 3)
    last_x = acum_x[ln - 1:ln, :]

    xdt = xs * dt_x
    xdt_b = xdt.astype(BF16)
    bm_b = bm.astype(BF16)
    cm_b = cm.astype(BF16)

    cb = lax.dot_general(cm_b, bm_b, (((1,), (1,)), ((), ())), preferred_element_type=F32)

    def head_diag(e, x_pair):
        seg = acum[:, e:e + 1] - acum_t[e:e + 1, :]
        decay = jnp.where(causal, jnp.exp(jnp.minimum(seg, 0.0)), 0.0)
        return jnp.dot((cb * decay).astype(BF16), x_pair, preferred_element_type=F32)

    first_half = lax.broadcasted_iota(jnp.int32, (ln, 2 * p), 1) < p
    parts = []
    for pair in range(gh // 2):
        x_pair = xdt_b[:, pair * 2 * p:(pair + 1) * 2 * p]
        parts.append(jnp.where(first_half, head_diag(2 * pair, x_pair), head_diag(2 * pair + 1, x_pair)))
    y_diag = jnp.concatenate(parts, axis=1)

    state = state_ref[...]
    y_off = jnp.dot(cm_b, state.astype(BF16), preferred_element_type=F32) * jnp.exp(acum_x)
    to_end = jnp.exp(last_x - acum_x)
    upd = jnp.dot(bm.T.astype(BF16), (xdt * to_end).astype(BF16), preferred_element_type=F32)
    state_ref[...] = state * jnp.exp(last_x) + upd

    y = y_diag + y_off + dskip_ref[...] * xs
    zf = z_ref[...].astype(F32)
    y = y * (zf * _sigmoid(zf))
    y = y * lax.rsqrt(jnp.mean(y * y, axis=-1, keepdims=True) + RMS_EPS)
    o_ref[...] = (y * nw_ref[...]).astype(o_ref.dtype)


def _ssd_mixer(h, dt_raw, conv_w, conv_b, dt_bias, a_log, d_skip, norm_w, batch, seq,
               z_block, x_block, b_block, c_block):
    ln = min(SSD_CHUNK, seq)
    nc = seq // ln
    hb = ln // HALO
    gd, gh = SSD_GROUP_DIM, SSD_GROUP_HEADS
    ngb = D_SSD // SSD_STATE

    def cur(base):
        return lambda b, g, c: (b * nc + c, base + g)

    def halo(base):
        return lambda b, g, c: (jnp.maximum((b * nc + c) * hb - 1, 0), base + g)

    grp = lambda b, g, c: (g, 0, 0)
    in_specs = [
        pl.BlockSpec((ln, gd), cur(z_block)),
        pl.BlockSpec((ln, gd), cur(x_block)), pl.BlockSpec((HALO, gd), halo(x_block)),
        pl.BlockSpec((ln, SSD_STATE), cur(b_block)), pl.BlockSpec((HALO, SSD_STATE), halo(b_block)),
        pl.BlockSpec((ln, SSD_STATE), cur(c_block)), pl.BlockSpec((HALO, SSD_STATE), halo(c_block)),
        pl.BlockSpec((None, ln, V7X_LANES), lambda b, g, c: (g, b * nc + c, 0)),
        pl.BlockSpec((SSD_CONV, gd), lambda b, g, c: (0, g)), pl.BlockSpec((1, gd), lambda b, g, c: (0, g)),
        pl.BlockSpec((SSD_CONV, SSD_STATE), lambda b, g, c: (0, ngb + g)),
        pl.BlockSpec((1, SSD_STATE), lambda b, g, c: (0, ngb + g)),
        pl.BlockSpec((SSD_CONV, SSD_STATE), lambda b, g, c: (0, ngb + SSD_GROUPS + g)),
        pl.BlockSpec((1, SSD_STATE), lambda b, g, c: (0, ngb + SSD_GROUPS + g)),
        pl.BlockSpec((None, 1, V7X_LANES), grp), pl.BlockSpec((None, 1, V7X_LANES), grp),
        pl.BlockSpec((1, gd), lambda b, g, c: (0, g)), pl.BlockSpec((1, gd), lambda b, g, c: (0, g)),
    ]
    conv_b2 = conv_b.reshape(1, -1)
    d_skip_x = jnp.repeat(d_skip.astype(F32), SSD_HEADDIM).reshape(1, D_SSD)

    def per_head(v):
        return jnp.pad(v.astype(F32).reshape(SSD_GROUPS, 1, gh), ((0, 0), (0, 0), (0, V7X_LANES - gh)))
    return pl.pallas_call(
        functools.partial(_ssd_body, ln=ln),
        grid=(batch, SSD_GROUPS, nc),
        in_specs=in_specs,
        out_specs=pl.BlockSpec((ln, gd), lambda b, g, c: (b * nc + c, g)),
        out_shape=jax.ShapeDtypeStruct((batch * seq, D_SSD), BF16),
        scratch_shapes=[pltpu.VMEM((HALO + ln, gd), BF16), pltpu.VMEM((SSD_STATE, gd), F32)],
        compiler_params=_params(("parallel", "parallel", "arbitrary")),
        name="ssd_mixer",
    )(h, h, h, h, h, h, h, dt_raw,
      conv_w, conv_b2, conv_w, conv_b2, conv_w, conv_b2,
      per_head(dt_bias), per_head(a_log), d_skip_x, norm_w.reshape(1, D_SSD))


def _gconv_body(bg_ref, cg_ref, cgh_ref, xi_ref, xih_ref, w_ref, o_ref, ext_ref, *, ts):
    i = pl.program_id(1)
    keep = jnp.where(i > 0, 1.0, 0.0).astype(F32)
    ext_ref[0:HALO, :] = cgh_ref[...].astype(F32) * xih_ref[...].astype(F32) * keep
    ext_ref[HALO:HALO + ts, :] = cg_ref[...].astype(F32) * xi_ref[...].astype(F32)
    acc = jnp.zeros((ts, D_CONV), F32)
    for tap in range(CONV_WIDTH):
        off = HALO - (CONV_WIDTH - 1) + tap
        acc = acc + w_ref[tap:tap + 1, :] * ext_ref[off:off + ts, :]
    o_ref[...] = (bg_ref[...].astype(F32) * acc).astype(o_ref.dtype)


def _gated_conv(h, conv_w, batch, seq):
    ts = min(CONV_TILE, seq)
    nt = seq // ts
    hb = ts // HALO

    def cur(blk):
        return lambda b, i: (b * nt + i, blk)

    def halo(blk):
        return lambda b, i: (jnp.maximum((b * nt + i) * hb - 1, 0), blk)

    return pl.pallas_call(
        functools.partial(_gconv_body, ts=ts),
        grid=(batch, nt),
        in_specs=[pl.BlockSpec((ts, D_CONV), cur(0)),
                  pl.BlockSpec((ts, D_CONV), cur(1)), pl.BlockSpec((HALO, D_CONV), halo(1)),
                  pl.BlockSpec((ts, D_CONV), cur(2)), pl.BlockSpec((HALO, D_CONV), halo(2)),
                  pl.BlockSpec((CONV_WIDTH, D_CONV), lambda b, i: (0, 0))],
        out_specs=pl.BlockSpec((ts, D_CONV), lambda b, i: (b * nt + i, 0)),
        out_shape=jax.ShapeDtypeStruct((batch * seq, D_CONV), BF16),
        scratch_shapes=[pltpu.VMEM((HALO + ts, D_CONV), F32)],
        compiler_params=_params(("parallel", "parallel")),
        name="gated_conv",
    )(h, h, h, h, h, conv_w)


def _sb_block(q, k_ref, v_ref, start, tk, carry, acc, upper, mask):
    kt = k_ref[pl.ds(start, tk), :]
    vt = v_ref[pl.ds(start, tk), :]
    z = lax.dot_general(q, kt, (((1,), (1,)), ((), ())), preferred_element_type=F32) * (SB_HEADDIM ** -0.5)
    sp = _softplus(z)
    log_stay = -sp if mask is None else jnp.where(mask, -sp, 0.0)
    hi = log_stay.astype(BF16)
    lo = (log_stay - hi.astype(F32)).astype(BF16)
    after = (jnp.dot(hi, upper, preferred_element_type=F32)
             + jnp.dot(lo, upper, preferred_element_type=F32)) + carry
    w = jnp.exp((z - sp) + after)
    if mask is not None:
        w = jnp.where(mask, w, 0.0)
    acc = acc + jnp.dot(w.astype(BF16), vt, preferred_element_type=F32)
    carry = carry + jnp.sum(log_stay, axis=1, keepdims=True)
    return carry, acc


def _sb_body(q_ref, k_ref, v_ref, o_ref, *, tq):
    qi = pl.program_id(2)
    q = q_ref[...]
    row = lax.broadcasted_iota(jnp.int32, (tq, tq), 0)
    col = lax.broadcasted_iota(jnp.int32, (tq, tq), 1)
    upper = jnp.where(row > col, 1.0, 0.0).astype(BF16)
    diag_mask = col < row

    def tile(j, carry, acc, mask):
        return _sb_block(q, k_ref, v_ref, pl.multiple_of(j * tq, tq), tq, carry, acc, upper, mask)

    def tile_pair(j, carry, acc):
        carry, acc = tile(j, carry, acc, None)
        has_second = jnp.broadcast_to(j >= 1, (tq, tq))
        return tile(jnp.maximum(j - 1, 0), carry, acc, has_second)

    carry = jnp.zeros((tq, 1), F32)
    acc = jnp.zeros((tq, SB_HEADDIM), F32)
    carry, acc = tile(qi, carry, acc, diag_mask)
    has_prev = jnp.broadcast_to(qi >= 1, (tq, tq))
    carry, acc = tile(jnp.maximum(qi - 1, 0), carry, acc, has_prev)

    def more(state):
        j, carry, _ = state
        return jnp.logical_and(j >= 0, jnp.max(carry) > SB_UNDERFLOW_LOG)

    def step(state):
        j, carry, acc = state
        carry, acc = tile_pair(j, carry, acc)
        return j - 2, carry, acc

    _, _, acc = lax.while_loop(more, step, (qi - 2, carry, acc))
    o_ref[...] = acc.astype(o_ref.dtype)


def _sb_attention(h, q_block, k_block, v_block, batch, seq):
    tq = min(SB_TILE, seq)
    nq = seq // tq
    return pl.pallas_call(
        functools.partial(_sb_body, tq=tq),
        grid=(batch, SB_HEADS, nq),
        in_specs=[pl.BlockSpec((tq, SB_HEADDIM), lambda b, hd, i: (b * nq + i, q_block + hd)),
                  pl.BlockSpec((seq, SB_HEADDIM), lambda b, hd, i: (b, k_block + hd)),
                  pl.BlockSpec((seq, SB_HEADDIM), lambda b, hd, i: (b, v_block + hd))],
        out_specs=pl.BlockSpec((tq, SB_HEADDIM), lambda b, hd, i: (b * nq + i, hd)),
        out_shape=jax.ShapeDtypeStruct((batch * seq, D_SB), BF16),
        compiler_params=_params(("parallel", "parallel", "arbitrary")),
        name="stick_breaking_attention",
    )(h, h, h)


def _router_body(x_ref, wt_ref, b_ref, idx_ref, wgt_ref):
    w_hi, w_lo = _bf16_terms(wt_ref[...], 2)
    x_hi, x_lo = _bf16_terms(x_ref[...], 2)
    nt_dot = lambda a, b: lax.dot_general(a, b, (((1,), (1,)), ((), ())), preferred_element_type=F32)
    logits = (nt_dot(w_hi, x_hi) + (nt_dot(w_hi, x_lo) + nt_dot(w_lo, x_hi))) + b_ref[...]
    iota = lax.broadcasted_iota(jnp.int32, logits.shape, 0).astype(F32)
    vals, idxs = [], []
    for _ in range(TOP_K):
        mx = jnp.max(logits, axis=0, keepdims=True)
        ix = jnp.min(jnp.where(logits == mx, iota, float(N_EXPERTS)), axis=0, keepdims=True)
        vals.append(mx)
        idxs.append(ix)
        logits = jnp.where(iota == ix, -jnp.inf, logits)
    exps = [jnp.exp(v - vals[0]) for v in vals]
    denom = exps[0] + exps[1] + exps[2] + exps[3]
    for k in range(TOP_K):
        idx_ref[k:k + 1, :] = idxs[k].astype(jnp.int32)
        wgt_ref[k:k + 1, :] = exps[k] / denom


def _router(x, router_w, router_b):
    t, d = x.shape
    tm = min(ROUTER_TILE, t)
    out = pl.BlockSpec((TOP_K, tm), lambda i: (0, i))
    return pl.pallas_call(
        _router_body,
        grid=(t // tm,),
        in_specs=[pl.BlockSpec((tm, d), lambda i: (i, 0)),
                  pl.BlockSpec((N_EXPERTS, d), lambda i: (0, 0)),
                  pl.BlockSpec((N_EXPERTS, 1), lambda i: (0, 0))],
        out_specs=[out, out],
        out_shape=[jax.ShapeDtypeStruct((TOP_K, t), jnp.int32), jax.ShapeDtypeStruct((TOP_K, t), F32)],
        compiler_params=_params(("parallel",)),
        name="moe_router",
    )(x, router_w.T, router_b.reshape(N_EXPERTS, 1))


def _routing_tables(idx, tm):
    t = idx.shape[1]
    pairs = TOP_K * t
    rows = pairs + N_EXPERTS * tm
    ntiles = rows // tm
    e_flat = idx.reshape(pairs)
    onehot = (e_flat[:, None] == jnp.arange(N_EXPERTS, dtype=jnp.int32)[None, :]).astype(jnp.int32)
    csum = jnp.cumsum(onehot, axis=0)
    rank = jnp.sum((csum - onehot) * onehot, axis=1)
    counts = csum[-1]
    padded = ((counts + tm - 1) // tm) * tm
    gend = jnp.cumsum(padded)
    gstart = gend - padded
    pos = (gstart[e_flat] + rank).astype(jnp.int32)
    tile_start = jnp.arange(ntiles, dtype=jnp.int32) * tm
    tile_expert = jnp.sum((tile_start[:, None] >= gend[None, :]).astype(jnp.int32), axis=1)
    tile_expert = jnp.minimum(tile_expert, N_EXPERTS - 1)
    tile_valid = (tile_start < gend[-1]).astype(jnp.int32)
    pos_tok = pos.reshape(TOP_K, t).T.reshape(pairs)
    last_tiles = jnp.where(counts > 0, gend - tm, -1)
    tail = tile_start[ntiles - N_EXPERTS:]
    tail_tiles = jnp.where(tail >= gend[-1], tail, -1)
    zero_rows = jnp.concatenate([last_tiles, tail_tiles]).astype(jnp.int32)
    return pos_tok, tile_expert, tile_valid, zero_rows


def _dispatch_body(pos_ref, zero_ref, x_ref, o_hbm, zbuf, sem, zsem, *, batch):
    i = pl.program_id(0)
    tile_rows = batch * ROW_SUBS

    @pl.when(i == 0)
    def _():
        zbuf[...] = jnp.zeros_like(zbuf)

        def zero_copy(j):
            start = pl.multiple_of(zero_ref[j] * ROW_SUBS, ROW_SUBS)
            return pltpu.make_async_copy(zbuf, o_hbm.at[pl.ds(start, tile_rows), :], zsem)

        for j in range(zero_ref.shape[0]):
            @pl.when(zero_ref[j] >= 0)
            def _():
                zero_copy(j).start()

        for j in range(zero_ref.shape[0]):
            @pl.when(zero_ref[j] >= 0)
            def _():
                zero_copy(j).wait()

    base = i * batch * TOP_K

    def body(r, carry):
        src = x_ref.at[pl.ds(pl.multiple_of(r * ROW_SUBS, ROW_SUBS), ROW_SUBS), :]
        for k in range(TOP_K):
            dst = pl.multiple_of(pos_ref[base + r * TOP_K + k] * ROW_SUBS, ROW_SUBS)
            pltpu.make_async_copy(src, o_hbm.at[pl.ds(dst, ROW_SUBS), :], sem).start()
        return carry

    lax.fori_loop(0, batch, body, 0, unroll=2)
    for _ in range(TOP_K):
        pltpu.make_async_copy(x_ref, o_hbm.at[pl.ds(0, tile_rows), :], sem).wait()


def _dispatch(x_packed, pos_tok, zero_rows, rows, batch):
    t = x_packed.shape[0] // ROW_SUBS
    grid_spec = pltpu.PrefetchScalarGridSpec(
        num_scalar_prefetch=2,
        grid=(t // batch,),
        in_specs=[pl.BlockSpec((batch * ROW_SUBS, V7X_LANES), lambda i, pos, zr: (i, 0))],
        out_specs=pl.BlockSpec(memory_space=pl.ANY),
        scratch_shapes=[pltpu.VMEM((batch * ROW_SUBS, V7X_LANES), jnp.uint32),
                        pltpu.SemaphoreType.DMA, pltpu.SemaphoreType.DMA],
    )
    return pl.pallas_call(
        functools.partial(_dispatch_body, batch=batch),
        grid_spec=grid_spec,
        out_shape=jax.ShapeDtypeStruct((rows * ROW_SUBS, V7X_LANES), jnp.uint32),
        compiler_params=_params(("arbitrary",)),
        name="moe_dispatch",
    )(pos_tok, zero_rows, x_packed)


def _expert_changed(te_ref, i):
    return jnp.logical_or(i == 0, te_ref[i] != te_ref[jnp.maximum(i - 1, 0)])


def _up_body(te_ref, tv_ref, x_ref, w1_ref, b1_ref, o_ref, w1b_ref):
    i = pl.program_id(0)
    tm = o_ref.shape[0]
    half = ROW_SUBS * V7X_LANES

    @pl.when(_expert_changed(te_ref, i))
    def _():
        w1b_ref[...] = w1_ref[...].astype(BF16)

    @pl.when(tv_ref[i] == 1)
    def _():
        low, high = _unpack_halves(_load_token_rows(x_ref, tm))
        hh = (jnp.dot(low.astype(BF16), w1b_ref[0:half, :], preferred_element_type=F32)
              + jnp.dot(high.astype(BF16), w1b_ref[half:2 * half, :], preferred_element_type=F32)) + b1_ref[...]
        glu = jnp.minimum(hh[:, :D_EXPERT], SWIGLU_LIMIT)
        lin = jnp.clip(hh[:, D_EXPERT:], -SWIGLU_LIMIT, SWIGLU_LIMIT)
        o_ref[...] = (glu * _sigmoid(SWIGLU_ALPHA * glu) * (lin + 1.0)).astype(o_ref.dtype)

    @pl.when(tv_ref[i] == 0)
    def _():
        o_ref[...] = jnp.zeros_like(o_ref)


def _down_body(te_ref, tv_ref, a_ref, w2_ref, b2_ref, o_ref, w2b_ref):
    i = pl.program_id(0)

    @pl.when(_expert_changed(te_ref, i))
    def _():
        w2b_ref[...] = w2_ref[...].astype(BF16)

    @pl.when(tv_ref[i] == 1)
    def _():
        out = jnp.dot(a_ref[...], w2b_ref[...], preferred_element_type=F32) + b2_ref[...]
        o_ref[...] = _pack_halves(out)

    @pl.when(tv_ref[i] == 0)
    def _():
        o_ref[...] = jnp.zeros_like(o_ref)


def _experts(x_sorted, w1, b1, w2, b2, tile_expert, tile_valid, tm):
    rows = x_sorted.shape[0] // ROW_SUBS
    half = ROW_SUBS * V7X_LANES
    d = 2 * half
    ntiles = rows // tm
    by_expert = lambda i, te, tv: (te[i], 0, 0)
    by_tile = lambda i, te, tv: (i, 0)
    packed_tile = pl.BlockSpec((tm * ROW_SUBS, V7X_LANES), by_tile)
    act = pl.pallas_call(
        _up_body,
        grid_spec=pltpu.PrefetchScalarGridSpec(
            num_scalar_prefetch=2,
            grid=(ntiles,),
            in_specs=[packed_tile,
                      pl.BlockSpec((None, d, 2 * D_EXPERT), by_expert),
                      pl.BlockSpec((None, 1, 2 * D_EXPERT), by_expert)],
            out_specs=pl.BlockSpec((tm, D_EXPERT), by_tile),
            scratch_shapes=[pltpu.VMEM((d, 2 * D_EXPERT), BF16)],
        ),
        out_shape=jax.ShapeDtypeStruct((rows, D_EXPERT), BF16),
        compiler_params=_params(("arbitrary",)),
        name="moe_up",
    )(tile_expert, tile_valid, x_sorted, w1, b1.reshape(N_EXPERTS, 1, -1))
    return pl.pallas_call(
        _down_body,
        grid_spec=pltpu.PrefetchScalarGridSpec(
            num_scalar_prefetch=2,
            grid=(ntiles,),
            in_specs=[pl.BlockSpec((tm, D_EXPERT), by_tile),
                      pl.BlockSpec((None, D_EXPERT, d), by_expert),
                      pl.BlockSpec((None, 1, d), by_expert)],
            out_specs=pl.BlockSpec((tm, half), by_tile),
            scratch_shapes=[pltpu.VMEM((D_EXPERT, d), BF16)],
        ),
        out_shape=jax.ShapeDtypeStruct((rows, half), jnp.uint32),
        compiler_params=_params(("arbitrary",)),
        name="moe_down",
    )(tile_expert, tile_valid, act, w2, b2.reshape(N_EXPERTS, 1, -1))


def _combine_body(pos_ref, y_hbm, x_ref, wgt_ref, g_ref, b_ref, of_ref, *rest, tm, emit_bf16):
    ob_ref = rest[0] if emit_bf16 else None
    ybuf, sem = rest[-2:]
    i = pl.program_id(0)
    nt = pl.num_programs(0)

    def row_copy(src, slot, k, r):
        return pltpu.make_async_copy(y_hbm.at[pl.ds(src, 1), :], ybuf.at[slot, k, pl.ds(r, 1), :], sem.at[slot])

    def issue(tile, slot):
        base = tile * tm * TOP_K

        def body(r, carry):
            for k in range(TOP_K):
                row_copy(pos_ref[base + r * TOP_K + k], slot, k, r).start(priority=k % DMA_QUEUES)
            return carry

        lax.fori_loop(0, tm, body, 0, unroll=4)

    @pl.when(i == 0)
    def _():
        issue(0, 0)

    @pl.when(i + 1 < nt)
    def _():
        issue(i + 1, (i + 1) % 2)

    slot = i % 2
    for k in range(TOP_K):
        pltpu.make_async_copy(y_hbm.at[pl.ds(0, tm), :], ybuf.at[slot, k], sem.at[slot]).wait()
    wgt = wgt_ref[...]
    moe_low, moe_high = None, None
    for k in range(TOP_K):
        low, high = _unpack_halves(ybuf[slot, k])
        gk = wgt[:, k:k + 1]
        moe_low = gk * low if k == 0 else moe_low + gk * low
        moe_high = gk * high if k == 0 else moe_high + gk * high
    moe = jnp.concatenate([moe_low, moe_high], axis=1)
    y = _layer_norm_rows(DEEPNORM_ALPHA * x_ref[...] + moe, g_ref[...], b_ref[...])
    of_ref[...] = y
    if emit_bf16:
        ob_ref[...] = y.astype(BF16)


def _combine_layer_norm(x, y_rows, pos_tok, wgt_tok, g, b, tm, emit_bf16):
    t, d = x.shape
    half = d // 2
    row = lambda i, pos: (i, 0)
    vec = lambda i, pos: (0, 0)
    out_specs = [pl.BlockSpec((tm, d), row)]
    out_shape = [jax.ShapeDtypeStruct((t, d), F32)]
    if emit_bf16:
        out_specs.append(pl.BlockSpec((tm, d), row))
        out_shape.append(jax.ShapeDtypeStruct((t, d), BF16))
    grid_spec = pltpu.PrefetchScalarGridSpec(
        num_scalar_prefetch=1,
        grid=(t // tm,),
        in_specs=[pl.BlockSpec(memory_space=pl.ANY),
                  pl.BlockSpec((tm, d), row), pl.BlockSpec((tm, TOP_K), row),
                  pl.BlockSpec((1, d), vec), pl.BlockSpec((1, d), vec)],
        out_specs=out_specs,
        scratch_shapes=[pltpu.VMEM((2, TOP_K, tm, half), jnp.uint32), pltpu.SemaphoreType.DMA((2,))],
    )
    return pl.pallas_call(
        functools.partial(_combine_body, tm=tm, emit_bf16=emit_bf16),
        grid_spec=grid_spec,
        out_shape=out_shape,
        compiler_params=_params(("arbitrary",)),
        name="moe_combine_layer_norm",
    )(pos_tok, y_rows, x, wgt_tok, g.reshape(1, d), b.reshape(1, d))


def _moe_block(x_f32, x_packed, router_w, router_b, w1, b1, w2, b2, g, b, emit_bf16):
    t = x_f32.shape[0]
    tm = min(MOE_TILE, t)
    idx, wgt = _router(x_f32, router_w, router_b)
    pos_tok, tile_expert, tile_valid, zero_rows = _routing_tables(idx, tm)
    x_sorted = _dispatch(x_packed, pos_tok, zero_rows, TOP_K * t + N_EXPERTS * tm, tm)
    y_rows = _experts(x_sorted, w1, b1, w2, b2, tile_expert, tile_valid, tm)
    return _combine_layer_norm(x_f32, y_rows, pos_tok, wgt.T, g, b, min(COMBINE_TILE, t), emit_bf16)


def _layer0_mixer(x_f32, x_b16, batch, seq, w_in, conv_w, conv_b, dt_bias, a_log, d_skip, norm_w,
                  pool_w, pool_scale, w_out, ln_g, ln_b):
    n_ssd = 2 * D_SSD + 2 * SSD_BC
    h_u = _matmul(x_b16, w_in, BF16, 1024, 512, 0, D_POOL)
    h = _matmul(x_b16, w_in, BF16, 1024, 512, D_POOL, n_ssd)
    w_dt = jnp.pad(w_in[:, D_POOL + n_ssd:], ((0, 0), (0, V7X_LANES - SSD_HEADS)))
    dt_raw = _matmul(x_b16, w_dt, F32, 1024, V7X_LANES)[:, :SSD_HEADS]
    t = batch * seq
    dt_raw = dt_raw.reshape(t, SSD_GROUPS, SSD_GROUP_HEADS).transpose(1, 0, 2)
    dt_raw = jnp.pad(dt_raw, ((0, 0), (0, 0), (0, V7X_LANES - SSD_GROUP_HEADS)))
    y_ssd = _ssd_mixer(h, dt_raw, conv_w, conv_b, dt_bias, a_log, d_skip, norm_w, batch, seq,
                       z_block=0, x_block=D_SSD // SSD_GROUP_DIM,
                       b_block=2 * D_SSD // SSD_STATE, c_block=(2 * D_SSD + SSD_BC) // SSD_STATE)
    y_pool = _pool_mixer(h_u, 0, pool_w.astype(BF16), pool_scale, batch, seq)
    m = _matmul_cat(y_pool, y_ssd, w_out, BF16, 1024, 1024, D_POOL)
    return _residual_layer_norm(x_f32, m, ln_g, ln_b)


def _layer1_mixer(x_f32, x_b16, batch, seq, w_in, conv_w, w_out, ln_g, ln_b):
    h = _matmul(x_b16, w_in, BF16, 1024, 512)
    y_conv = _gated_conv(h, conv_w, batch, seq)
    qb = 3 * D_CONV // SB_HEADDIM
    y_sb = _sb_attention(h, qb, qb + SB_HEADS, qb + 2 * SB_HEADS, batch, seq)
    m = _matmul_cat(y_conv, y_sb, w_out, BF16, 1024, 1024, D_CONV)
    return _residual_layer_norm(x_f32, m, ln_g, ln_b)


def kernel(x, l0_w_in, l0_conv_w, l0_conv_b, l0_dt_bias, l0_a_log, l0_d_skip, l0_ssm_norm_w, l0_pool_w, l0_pool_scale, l0_w_out, l0_ln_mix_g, l0_ln_mix_b, l0_router_w, l0_router_b, l0_w1, l0_b1, l0_w2, l0_b2, l0_ln_ffn_g, l0_ln_ffn_b, l1_w_in, l1_conv_w, l1_w_out, l1_ln_mix_g, l1_ln_mix_b, l1_router_w, l1_router_b, l1_w1, l1_b1, l1_w2, l1_b2, l1_ln_ffn_g, l1_ln_ffn_b):
    batch, seq, d = x.shape
    xf = x.reshape(batch * seq, d)
    xb = xf.astype(BF16)
    xf, xp = _layer0_mixer(xf, xb, batch, seq, l0_w_in, l0_conv_w, l0_conv_b, l0_dt_bias, l0_a_log, l0_d_skip,
                           l0_ssm_norm_w, l0_pool_w, l0_pool_scale, l0_w_out, l0_ln_mix_g, l0_ln_mix_b)
    xf, xb = _moe_block(xf, xp, l0_router_w, l0_router_b, l0_w1, l0_b1, l0_w2, l0_b2, l0_ln_ffn_g, l0_ln_ffn_b,
                        emit_bf16=True)
    xf, xp = _layer1_mixer(xf, xb, batch, seq, l1_w_in, l1_conv_w, l1_w_out, l1_ln_mix_g, l1_ln_mix_b)
    (xf,) = _moe_block(xf, xp, l1_router_w, l1_router_b, l1_w1, l1_b1, l1_w2, l1_b2, l1_ln_ffn_g, l1_ln_ffn_b,
                       emit_bf16=False)
    return xf.reshape(batch, seq, d)
```
